```python
import math
import jax
import jax.numpy as jnp
from jax import lax
import numpy as np

D_MODEL = 2048
BATCH = 4
SEQ = 2048
DEPTH = 4

GRID_W = 64
CTX_LEN = 256
HEAD_DIM = 128
HY_DIM = D_MODEL // 2
HY_ORDER = 2
HY_SHORT = 3
HY_BANDS = 16
HY_EMB = 2 * HY_BANDS + 1
HY_FILT_HID = 64
HY_FILT_INNER = 2
HY_FILT_SCALE = 0.1
HY_FAST_DECAY = 0.3
HY_SLOW_DECAY = 1.5
HY_TARGET = 1e-2
SWA_HEADS = (D_MODEL - HY_DIM) // HEAD_DIM
SWA_KV_HEADS = SWA_HEADS // 4
SWA_WINDOW = 128
SWA_BLOCK = 128
NA_HEADS = D_MODEL // HEAD_DIM
NA_KH = 8
NA_KW = 16
ROPE_BASE = 10000.0
N_EXPERTS = 16
EC_CAPACITY = 2
EXPERT_FF = D_MODEL // 2
N_EVEN = (DEPTH + 1) // 2
N_ODD = DEPTH // 2
DN_ALPHA = (2 * DEPTH) ** 0.25
DN_BETA = (8 * DEPTH) ** -0.25
LN_EPS = 1e-5
EV_Q0 = 3 * HY_DIM
EV_K0 = EV_Q0 + SWA_HEADS * HEAD_DIM
EV_V0 = EV_K0 + SWA_KV_HEADS * HEAD_DIM
EV_IN = EV_V0 + SWA_KV_HEADS * HEAD_DIM
EV_CAT = HY_DIM + SWA_HEADS * HEAD_DIM
F32 = jnp.float32

kernel_name = 'hybrid_hyena_swa_natten_ec_dit'


def layer_norm(x, g, b):
    xf = x.astype(F32)
    mu = jnp.mean(xf, axis=-1, keepdims=True)
    var = jnp.mean(jnp.square(xf - mu), axis=-1, keepdims=True)
    y = (xf - mu) * lax.rsqrt(var + LN_EPS) * g.astype(F32) + b.astype(F32)
    return y.astype(x.dtype)


def axial_rope(x):
    L, d = x.shape[1], x.shape[-1]
    t = jnp.arange(L)
    half = d // 2
    nf = half // 2
    inv = ROPE_BASE ** (-2.0 * jnp.arange(nf, dtype=F32) / half)
    parts = []
    for a, pos in enumerate((t // GRID_W, t % GRID_W)):
        xa = x[..., a * half:(a + 1) * half].astype(F32)
        ang = pos.astype(F32)[:, None] * inv[None, :]
        cos = jnp.cos(ang)[None, :, None, :]
        sin = jnp.sin(ang)[None, :, None, :]
        x1, x2 = xa[..., :nf], xa[..., nf:]
        parts.append(jnp.concatenate([x1 * cos - x2 * sin, x2 * cos + x1 * sin], axis=-1))
    return jnp.concatenate(parts, axis=-1).astype(x.dtype)


def short_conv(u, w, b):
    L = u.shape[1]
    p = HY_SHORT // 2
    up = jnp.pad(u, ((0, 0), (p, HY_SHORT - 1 - p), (0, 0)))
    y = b
    for j in range(HY_SHORT):
        y = y + up[:, j:j + L] * w[j]
    return y


def hyena_filters(L, fw1, fb1, fw2, fb2, fw3, ffreq):
    t = jnp.linspace(0.0, 1.0, L, dtype=F32)[:, None]
    w = (2.0 * math.pi / L) * jnp.arange(L, dtype=F32)[:, None]
    f = jnp.linspace(1e-4, HY_BANDS - 1, HY_BANDS, dtype=F32)[None, :]
    z = jnp.concatenate([t, jnp.cos(f * w), -jnp.sin(f * w)], axis=-1)
    fr = ffreq.astype(F32)
    h = jnp.sin(fr * (z @ fw1.astype(F32) + fb1.astype(F32)))
    for i in range(HY_FILT_INNER):
        h = jnp.sin(fr * (h @ fw2[i].astype(F32) + fb2[i].astype(F32)))
    h = (h @ fw3.astype(F32)).reshape(L, HY_ORDER, 2, HY_DIM)
    deltas = jnp.abs(jnp.linspace(math.log(HY_TARGET) / HY_SLOW_DECAY, math.log(HY_TARGET) / HY_FAST_DECAY, HY_DIM, dtype=F32))
    decay = jnp.exp(-t * deltas[None, :])
    return h * decay[:, None, None, :]


def bidir_long_conv(z, hf, hb, bias):
    B, L, C = z.shape
    kc = jnp.concatenate([hf.at[0].add(hb[0]), jnp.zeros((1, C), F32), hb[:0:-1]], axis=0)
    zf = jnp.fft.rfft(z.astype(F32), n=2 * L, axis=1)
    kf = jnp.fft.rfft(kc, n=2 * L, axis=0)
    y = jnp.fft.irfft(zf * kf[None], n=2 * L, axis=1)[:, :L]
    return (y + z.astype(F32) * bias.astype(F32)).astype(z.dtype)


def hyena_mix(u, conv_w, conv_b, filt, hy_bias):
    L = u.shape[1]
    u = short_conv(u, conv_w, conv_b)
    parts = jnp.split(u, HY_ORDER + 1, axis=-1)
    h = hyena_filters(L, *filt)
    z = parts[-1]
    for n in range(HY_ORDER):
        z = parts[n] * bidir_long_conv(z, h[:, n, 0], h[:, n, 1], hy_bias[n])
    return z


def window_gqa_attn(q, k, v, kc, vc, sink):
    B, L, Hq, d = q.shape
    Hkv = k.shape[2]
    G = Hq // Hkv
    W = SWA_BLOCK
    nb = L // W
    scale = d ** -0.5
    qb = q.reshape(B, nb, W, Hkv, G, d)

    def band(a):
        ap = jnp.pad(a, ((0, 0), (W, W), (0, 0), (0, 0))).reshape(B, nb + 2, W, Hkv, d)
        return jnp.concatenate([ap[:, :-2], ap[:, 1:-1], ap[:, 2:]], axis=2)

    kb, vb = band(k), band(v)
    s_loc = jnp.einsum('bnqhgd,bnkhd->bnhgqk', qb, kb).astype(F32) * scale
    qpos = jnp.arange(nb)[:, None] * W + jnp.arange(W)[None, :]
    kpos = jnp.arange(nb)[:, None] * W - W + jnp.arange(3 * W)[None, :]
    ok = (jnp.abs(qpos[:, :, None] - kpos[:, None, :]) <= SWA_WINDOW) & (kpos[:, None, :] >= 0) & (kpos[:, None, :] < L)
    s_loc = jnp.where(ok[None, :, None, None], s_loc, -jnp.inf)
    s_ctx = jnp.einsum('bnqhgd,bchd->bnhgqc', qb, kc).astype(F32) * scale
    s_snk = jnp.broadcast_to(sink.astype(F32).reshape(Hkv, G)[None, None, :, :, None, None], s_loc.shape[:-1] + (1,))
    p = jax.nn.softmax(jnp.concatenate([s_loc, s_ctx, s_snk], axis=-1), axis=-1).astype(v.dtype)
    nk = 3 * W
    nc = kc.shape[1]
    o = jnp.einsum('bnhgqk,bnkhd->bnqhgd', p[..., :nk], vb) + jnp.einsum('bnhgqc,bchd->bnqhgd', p[..., nk:nk + nc], vc)
    return o.reshape(B, L, Hq * d)


def ctx_attn(q, k, v, sink):
    B, Lc, Hq, d = q.shape
    Hkv = k.shape[2]
    G = Hq // Hkv
    qg = q.reshape(B, Lc, Hkv, G, d)
    s = jnp.einsum('bqhgd,bkhd->bhgqk', qg, k).astype(F32) * d ** -0.5
    if sink is not None:
        snk = jnp.broadcast_to(sink.astype(F32).reshape(Hkv, G)[None, :, :, None, None], s.shape[:-1] + (1,))
        s = jnp.concatenate([s, snk], axis=-1)
    p = jax.nn.softmax(s, axis=-1)[..., :Lc].astype(v.dtype)
    return jnp.einsum('bhgqk,bkhd->bqhgd', p, v).reshape(B, Lc, Hq * d)


def neighbourhood_attn(q, k, v, kc, vc, rpb):
    B, L, H, d = q.shape
    rows = L // GRID_W
    kh = min(NA_KH, rows)
    kw = min(NA_KW, GRID_W)
    scale = d ** -0.5
    qg = jnp.moveaxis(q.reshape(B, rows, GRID_W, H, d), 1, 0)
    kg = k.reshape(B, rows, GRID_W, H, d)
    vg = v.reshape(B, rows, GRID_W, H, d)
    col = jnp.arange(GRID_W)
    cs = jnp.clip(col - kw // 2, 0, GRID_W - kw)
    col_ok = (col[None, :] >= cs[:, None]) & (col[None, :] < cs[:, None] + kw)
    dc_idx = jnp.clip(col[None, :] - col[:, None] + NA_KW - 1, 0, 2 * NA_KW - 2)
    rpb32 = rpb.astype(F32)
    n = kh * GRID_W

    def row_block(args):
        r, qr = args
        rs = jnp.clip(r - kh // 2, 0, rows - kh)
        kb = lax.dynamic_slice_in_dim(kg, rs, kh, axis=1)
        vb = lax.dynamic_slice_in_dim(vg, rs, kh, axis=1).reshape(B, n, H, d)
        dr_idx = rs - r + jnp.arange(kh) + NA_KH - 1
        bias = rpb32[:, dr_idx[None, :, None], dc_idx[:, None, :]]
        s = jnp.einsum('bqhd,brkhd->bhqrk', qr, kb).astype(F32) * scale + bias[None]
        s = jnp.where(col_ok[:, None, :], s, -jnp.inf).reshape(B, H, GRID_W, n)
        sc = jnp.einsum('bqhd,bchd->bhqc', qr, kc).astype(F32) * scale
        p = jax.nn.softmax(jnp.concatenate([s, sc], axis=-1), axis=-1).astype(v.dtype)
        return jnp.einsum('bhqk,bkhd->bqhd', p[..., :n], vb) + jnp.einsum('bhqc,bchd->bqhd', p[..., n:], vc)

    out = lax.map(row_block, (jnp.arange(rows), qg))
    return jnp.moveaxis(out, 0, 1).reshape(B, L, H * d)


def expert_choice_ffn(h, w_router, w1, w3, w2):
    B, N, D = h.shape
    cap = EC_CAPACITY * N // N_EXPERTS
    aff = jax.nn.softmax((h @ w_router).astype(F32), axis=-1)
    gate, idx = lax.top_k(jnp.swapaxes(aff, 1, 2), cap)
    bidx = jnp.arange(B)[:, None, None]
    xs = h[bidx, idx]
    a = jnp.einsum('becd,edf->becf', xs, w1)
    u = jnp.einsum('becd,edf->becf', xs, w3)
    y = jnp.einsum('becf,efd->becd', jax.nn.silu(a) * u, w2) * gate[..., None].astype(h.dtype)
    return jnp.zeros_like(h).at[bidx, idx].add(y)


def even_mixer(h_lat, h_ctx, w_in, w_out, conv_w, conv_b, filt, hy_bias, sink, need_ctx):
    B, L, _ = h_lat.shape
    Lc = h_ctx.shape[1]
    p = h_lat @ w_in
    q = axial_rope(p[..., EV_Q0:EV_K0].reshape(B, L, SWA_HEADS, HEAD_DIM))
    k = axial_rope(p[..., EV_K0:EV_V0].reshape(B, L, SWA_KV_HEADS, HEAD_DIM))
    v = p[..., EV_V0:].reshape(B, L, SWA_KV_HEADS, HEAD_DIM)
    off = 0 if need_ctx else EV_K0
    pc = h_ctx @ w_in[:, off:]
    kc = pc[..., EV_K0 - off:EV_V0 - off].reshape(B, Lc, SWA_KV_HEADS, HEAD_DIM)
    vc = pc[..., EV_V0 - off:].reshape(B, Lc, SWA_KV_HEADS, HEAD_DIM)
    a = window_gqa_attn(q, k, v, kc, vc, sink)
    y_lat = jnp.concatenate([hyena_mix(p[..., :EV_Q0], conv_w, conv_b, filt, hy_bias), a], axis=-1) @ w_out
    if not need_ctx:
        return y_lat, None
    qc = pc[..., EV_Q0:EV_K0].reshape(B, Lc, SWA_HEADS, HEAD_DIM)
    ac = ctx_attn(qc, kc, vc, sink)
    y_ctx = jnp.concatenate([hyena_mix(pc[..., :EV_Q0], conv_w, conv_b, filt, hy_bias), ac], axis=-1) @ w_out
    return y_lat, y_ctx


def odd_mixer(h_lat, h_ctx, w_in, w_out, rpb, need_ctx):
    B, L, _ = h_lat.shape
    Lc = h_ctx.shape[1]
    hd = NA_HEADS * HEAD_DIM
    p = (h_lat @ w_in).reshape(B, L, 3, NA_HEADS, HEAD_DIM)
    off = 0 if need_ctx else hd
    pc = (h_ctx @ w_in[:, off:]).reshape(B, Lc, -1, NA_HEADS, HEAD_DIM)
    kc, vc = pc[:, :, -2], pc[:, :, -1]
    y_lat = neighbourhood_attn(p[:, :, 0], p[:, :, 1], p[:, :, 2], kc, vc, rpb) @ w_out
    if not need_ctx:
        return y_lat, None
    y_ctx = ctx_attn(pc[:, :, 0], kc, vc, None) @ w_out
    return y_lat, y_ctx


def setup_inputs(seed: int = 0) -> dict:
    key = jax.random.key(seed)
    keys = jax.random.split(key, 32)
    counter = iter(range(32))

    def nrm(shape, scale):
        return scale * jax.random.normal(keys[next(counter)], shape, dtype=F32)

    D, E, F = D_MODEL, N_EXPERTS, EXPERT_FF
    return {
        'x': nrm((BATCH, SEQ, D), 1.0),
        'c': nrm((BATCH, D), 1.0),
        'ctx': nrm((BATCH, CTX_LEN, D), 1.0),
        'c_ctx': nrm((D,), 1.0),
        'ada_w': nrm((DEPTH, D, 6 * D), 0.5 * D ** -0.5),
        'ada_b': nrm((DEPTH, 6 * D), 0.01),
        'ln_g': 1.0 + nrm((DEPTH, 2, D), 0.02),
        'ln_b': nrm((DEPTH, 2, D), 0.02),
        'ev_w_in': nrm((N_EVEN, D, EV_IN), D ** -0.5),
        'ev_w_out': nrm((N_EVEN, EV_CAT, D), DN_BETA * EV_CAT ** -0.5),
        'hy_conv_w': nrm((N_EVEN, HY_SHORT, (HY_ORDER + 1) * HY_DIM), HY_SHORT ** -0.5),
        'hy_conv_b': nrm((N_EVEN, (HY_ORDER + 1) * HY_DIM), 0.02),
        'hy_f_w1': nrm((N_EVEN, HY_EMB, HY_FILT_HID), HY_EMB ** -0.5),
        'hy_f_b1': nrm((N_EVEN, HY_FILT_HID), 0.1),
        'hy_f_w2': nrm((N_EVEN, HY_FILT_INNER, HY_FILT_HID, HY_FILT_HID), HY_FILT_HID ** -0.5),
        'hy_f_b2': nrm((N_EVEN, HY_FILT_INNER, HY_FILT_HID), 0.1),
        'hy_f_w3': nrm((N_EVEN, HY_FILT_HID, HY_ORDER * 2 * HY_DIM), HY_FILT_SCALE * HY_FILT_HID ** -0.5),
        'hy_f_freq': 1.0 + nrm((N_EVEN, HY_FILT_HID), 0.1),
        'hy_bias': nrm((N_EVEN, HY_ORDER, HY_DIM), 0.1),
        'swa_sink': nrm((N_EVEN, SWA_HEADS), 0.5),
        'od_w_in': nrm((N_ODD, D, 3 * NA_HEADS * HEAD_DIM), D ** -0.5),
        'od_w_out': nrm((N_ODD, NA_HEADS * HEAD_DIM, D), DN_BETA * (NA_HEADS * HEAD_DIM) ** -0.5),
        'na_rpb': nrm((N_ODD, NA_HEADS, 2 * NA_KH - 1, 2 * NA_KW - 1), 0.02),
        'moe_w_router': nrm((DEPTH, D, E), D ** -0.5),
        'moe_w1': nrm((DEPTH, E, D, F), D ** -0.5),
        'moe_w3': nrm((DEPTH, E, D, F), D ** -0.5),
        'moe_w2': nrm((DEPTH, E, F, D), DN_BETA * F ** -0.5),
    }


def reference(x, c, ctx, c_ctx, ada_w, ada_b, ln_g, ln_b, ev_w_in, ev_w_out, hy_conv_w, hy_conv_b,
              hy_f_w1, hy_f_b1, hy_f_w2, hy_f_b2, hy_f_w3, hy_f_freq, hy_bias, swa_sink,
              od_w_in, od_w_out, na_rpb, moe_w_router, moe_w1, moe_w3, moe_w2):
    B, L, D = x.shape
    s_lat = jax.nn.silu(c)
    s_ctx = jax.nn.silu(c_ctx)
    x_lat, x_ctx = x, ctx
    for l in range(DEPTH):
        need_ctx = l < DEPTH - 1
        m_lat = (s_lat @ ada_w[l] + ada_b[l]).reshape(B, 6, 1, D)
        m_ctx = (s_ctx @ ada_w[l] + ada_b[l]).reshape(6, 1, 1, D)
        h_lat = x_lat * (1.0 + m_lat[:, 1]) + m_lat[:, 0]
        h_ctx = x_ctx * (1.0 + m_ctx[1]) + m_ctx[0]
        i = l // 2
        if l % 2 == 0:
            filt = (hy_f_w1[i], hy_f_b1[i], hy_f_w2[i], hy_f_b2[i], hy_f_w3[i], hy_f_freq[i])
            y_lat, y_ctx = even_mixer(h_lat, h_ctx, ev_w_in[i], ev_w_out[i], hy_conv_w[i], hy_conv_b[i],
                                      filt, hy_bias[i], swa_sink[i], need_ctx)
        else:
            y_lat, y_ctx = odd_mixer(h_lat, h_ctx, od_w_in[i], od_w_out[i], na_rpb[i], need_ctx)
        x_lat = layer_norm(DN_ALPHA * x_lat + m_lat[:, 2] * y_lat, ln_g[l, 0], ln_b[l, 0])
        f_lat = expert_choice_ffn(x_lat * (1.0 + m_lat[:, 4]) + m_lat[:, 3], moe_w_router[l], moe_w1[l], moe_w3[l], moe_w2[l])
        x_lat = layer_norm(DN_ALPHA * x_lat + m_lat[:, 5] * f_lat, ln_g[l, 1], ln_b[l, 1])
        if need_ctx:
            x_ctx = layer_norm(DN_ALPHA * x_ctx + m_ctx[2] * y_ctx, ln_g[l, 0], ln_b[l, 0])
            f_ctx = expert_choice_ffn(x_ctx * (1.0 + m_ctx[4]) + m_ctx[3], moe_w_router[l], moe_w1[l], moe_w3[l], moe_w2[l])
            x_ctx = layer_norm(DN_ALPHA * x_ctx + m_ctx[5] * f_ctx, ln_g[l, 1], ln_b[l, 1])
    return x_lat
```

```python
import functools
import math

import jax
import jax.numpy as jnp
from jax import lax
from jax.experimental import pallas as pl
from jax.experimental.pallas import tpu as pltpu

F32 = jnp.float32
BF16 = jnp.bfloat16
I32 = jnp.int32

D = 2048
NB = 4
L = 2048
LC = 256
S = L + LC
DEPTH = 4
GW = 64
HD = 128
HY = D // 2
HY_BANDS = 16
HY_EMB = 2 * HY_BANDS + 1
HY_HID = 64
HY_FAST_DECAY = 0.3
HY_SLOW_DECAY = 1.5
HY_TARGET = 1e-2
SWA_H = (D - HY) // HD
SWA_KV = SWA_H // 4
SWA_G = SWA_H // SWA_KV
SWA_W = 128
NA_H = D // HD
NA_KH = 8
NA_KW = 16
ROPE_BASE = 10000.0
NE = 16
CAP_L = 2 * L // NE
CAP_C = 2 * LC // NE
CAP = CAP_L + CAP_C
FF = D // 2
DN_ALPHA = (2 * DEPTH) ** 0.25
LN_EPS = 1e-5
EV_Q0 = 3 * HY
EV_K0 = EV_Q0 + SWA_H * HD
EV_IN = EV_K0 + 2 * SWA_KV * HD
NEG = -1e30

TM = 256
N_LAT_TILES = L // TM
VMEM_LIMIT = 56 * 1024 * 1024


def _params(n_axes, vmem=VMEM_LIMIT):
    return pltpu.CompilerParams(dimension_semantics=("arbitrary",) * n_axes, vmem_limit_bytes=vmem)


def _mod_spec(layer, k):
    def index(b, i):
        row = jnp.where(i >= N_LAT_TILES, NB, b)
        return ((layer * 8 + row) * 6 + k, 0, 0)
    return pl.BlockSpec((1, 1, D), index)


def _ada_kernel(c_ref, w_ref, b_ref, o_ref):
    c = c_ref[...]
    s = c * jax.nn.sigmoid(c)
    o_ref[0] = jnp.dot(s.astype(BF16), w_ref[0].astype(BF16), preferred_element_type=F32) + b_ref[0]


def _ada(cc, ada_w, ada_b):
    tn = 1024
    return pl.pallas_call(
        _ada_kernel, grid=(DEPTH, 6 * D // tn),
        in_specs=[pl.BlockSpec((8, D), lambda l, j: (0, 0)),
                  pl.BlockSpec((1, D, tn), lambda l, j: (l, 0, j)),
                  pl.BlockSpec((1, 1, tn), lambda l, j: (l, 0, j))],
        out_specs=pl.BlockSpec((1, 8, tn), lambda l, j: (l, 0, j)),
        out_shape=jax.ShapeDtypeStruct((DEPTH, 8, 6 * D), F32),
        compiler_params=_params(2), name="ada")(cc, ada_w, ada_b.reshape(DEPTH, 1, 6 * D))


def _mod_kernel(x_ref, sc_ref, sh_ref, o_ref):
    o_ref[0] = (x_ref[0] * (1.0 + sc_ref[0]) + sh_ref[0]).astype(o_ref.dtype)


def _modulate(x, mod, layer):
    return pl.pallas_call(
        _mod_kernel, grid=(NB, S // TM),
        in_specs=[pl.BlockSpec((1, TM, D), lambda b, i: (b, i, 0)), _mod_spec(layer, 1), _mod_spec(layer, 0)],
        out_specs=pl.BlockSpec((1, TM, D), lambda b, i: (b, i, 0)),
        out_shape=jax.ShapeDtypeStruct((NB, S, D), BF16),
        compiler_params=_params(2), name="modulate")(x, mod, mod)


def _swap32(x):
    n = x.shape[1]
    lane = lax.broadcasted_iota(I32, x.shape, 1)
    return jnp.where((lane & 32) == 0, pltpu.roll(x, n - 32, 1), pltpu.roll(x, 32, 1))


def _proj_kernel(a_ref, w_ref, cos_ref, sin_ref, o_ref, wbf_ref, *, rope_cols):
    @pl.when((pl.program_id(1) == 0) & (pl.program_id(2) == 0))
    def _():
        wbf_ref[...] = w_ref[...].astype(BF16)

    acc = jnp.dot(a_ref[0], wbf_ref[...], preferred_element_type=F32)
    if rope_cols:
        xr = acc[:, :rope_cols]
        reps = rope_cols // HD
        cos = jnp.concatenate([cos_ref[...]] * reps, axis=1)
        sin = jnp.concatenate([sin_ref[...]] * reps, axis=1)
        xr = xr * cos + _swap32(xr) * sin
        acc = xr if rope_cols == acc.shape[1] else jnp.concatenate([xr, acc[:, rope_cols:]], axis=1)
    o_ref[0] = acc.astype(o_ref.dtype)


def _proj(h, w, layer, col0, ncols, tn, rope_cols, rope, out_dtype, name):
    assert col0 % tn == 0 and ncols % tn == 0
    off = col0 // tn
    cos_t, sin_t = rope
    return pl.pallas_call(
        functools.partial(_proj_kernel, rope_cols=rope_cols),
        grid=(ncols // tn, NB, S // TM),
        in_specs=[pl.BlockSpec((1, TM, D), lambda j, b, i: (b, i, 0)),
                  pl.BlockSpec((None, D, tn), lambda j, b, i: (layer, 0, j + off)),
                  pl.BlockSpec((TM, HD), lambda j, b, i: (i, 0)),
                  pl.BlockSpec((TM, HD), lambda j, b, i: (i, 0))],
        out_specs=pl.BlockSpec((1, TM, tn), lambda j, b, i: (b, i, j)),
        out_shape=jax.ShapeDtypeStruct((NB, S, ncols), out_dtype),
        scratch_shapes=[pltpu.VMEM((D, tn), BF16)],
        compiler_params=_params(3), name=name)(h, w, cos_t, sin_t)


def _rope_tables():
    t = jnp.arange(S)
    nf = HD // 4
    inv = ROPE_BASE ** (-2.0 * jnp.arange(nf, dtype=F32) / (HD // 2))
    lat = (t < L)[:, None]
    cos, sin = [], []
    for pos in (t // GW, t % GW):
        ang = pos.astype(F32)[:, None] * inv[None, :]
        c = jnp.where(lat, jnp.cos(ang), 1.0)
        s = jnp.where(lat, jnp.sin(ang), 0.0)
        cos += [c, c]
        sin += [-s, s]
    return jnp.concatenate(cos, axis=1), jnp.concatenate(sin, axis=1)


def _swa_kernel(sink_ref, q_ref, k0_ref, k1_ref, k2_ref, kc_ref, v0_ref, v1_ref, v2_ref, vc_ref, o_ref):
    n = pl.program_id(1)
    hk = pl.program_id(2)
    q = q_ref[0]
    qs = jnp.concatenate([q[:, g * HD:(g + 1) * HD] for g in range(SWA_G)], axis=0)
    k = jnp.concatenate([k0_ref[0], k1_ref[0], k2_ref[0], kc_ref[0]], axis=0)
    v = jnp.concatenate([v0_ref[0], v1_ref[0], v2_ref[0], vc_ref[0]], axis=0)
    s = lax.dot_general(qs, k, (((1,), (1,)), ((), ())), preferred_element_type=F32) * (HD ** -0.5)
    rows, cols = s.shape
    ri = lax.broadcasted_iota(I32, (rows, cols), 0)
    ci = lax.broadcasted_iota(I32, (rows, cols), 1)
    qpos = n * SWA_W + (ri & (SWA_W - 1))
    kpos = (n - 1) * SWA_W + ci
    band_ok = (jnp.abs(qpos - kpos) <= SWA_W) & (kpos >= 0) & (kpos < L) & (n < L // SWA_W)
    s = jnp.where((ci >= 3 * SWA_W) | band_ok, s, NEG)
    rg = lax.broadcasted_iota(I32, (rows, 1), 0) // SWA_W
    snk = jnp.zeros((rows, 1), F32)
    for g in range(SWA_G):
        snk = jnp.where(rg == g, sink_ref[hk * SWA_G + g], snk)
    m = jnp.maximum(jnp.max(s, axis=1, keepdims=True), snk)
    p = jnp.exp(s - m)
    den = jnp.sum(p, axis=1, keepdims=True) + jnp.exp(snk - m)
    o = jnp.dot(p.astype(BF16), v, preferred_element_type=F32) / den
    o_ref[0] = jnp.concatenate([o[g * SWA_W:(g + 1) * SWA_W] for g in range(SWA_G)], axis=1).astype(o_ref.dtype)


def _swa(q, kv, sink):
    nb = L // SWA_W
    nq = S // SWA_W
    cblk = L // LC

    def band(j, col0):
        return pl.BlockSpec((1, SWA_W, HD), lambda b, n, hk: (b, jnp.clip(n + j - 1, 0, nb - 1), col0 + hk))

    def ctx(col0):
        return pl.BlockSpec((1, LC, HD), lambda b, n, hk: (b, cblk, col0 + hk))

    qspec = pl.BlockSpec((1, SWA_W, SWA_G * HD), lambda b, n, hk: (b, n, hk))
    return pl.pallas_call(
        _swa_kernel, grid=(NB, nq, SWA_KV),
        in_specs=[pl.BlockSpec(memory_space=pltpu.SMEM), qspec,
                  band(0, 0), band(1, 0), band(2, 0), ctx(0),
                  band(0, SWA_KV), band(1, SWA_KV), band(2, SWA_KV), ctx(SWA_KV)],
        out_specs=qspec,
        out_shape=jax.ShapeDtypeStruct((NB, S, SWA_H * HD), BF16),
        compiler_params=_params(3), name="swa")(sink, q, kv, kv, kv, kv, kv, kv, kv, kv)


N_DR = 2 * NA_KH - 1
N_DC = 2 * NA_KW - 1
NA_G = 4
NA_WIN = NA_G + NA_KH
NA_QB = NA_G * GW
NA_KB = NA_WIN * GW // NA_QB


def _na_bias_kernel(rpb_ref, o_ref):
    h = pl.program_id(0)
    shape = (GW, 2 * GW)
    lane = lax.broadcasted_iota(I32, shape, 1)
    qc = lax.broadcasted_iota(I32, shape, 0)
    kc = lane & (GW - 1)
    diff = kc - qc + (NA_KW - 1)
    cs = jnp.clip(qc - NA_KW // 2, 0, GW - NA_KW)
    ok = (kc >= cs) & (kc < cs + NA_KW)
    right = lax.broadcasted_iota(I32, (1, 2 * GW), 1) >= GW
    for dd in range(N_DR + 1):
        acc = jnp.zeros(shape, F32)
        for m in range(N_DC):
            lo = rpb_ref[(h * N_DR + dd - 1) * N_DC + m] if dd >= 1 else 0.0
            hi = rpb_ref[(h * N_DR + dd) * N_DC + m] if dd < N_DR else 0.0
            acc = jnp.where(diff == m, jnp.where(right, hi, lo), acc)
        valid = ok & (right if dd == 0 else (~right if dd == N_DR else True))
        o_ref[0, dd] = jnp.where(valid, acc, NEG)


def _na_bias(rpb):
    return pl.pallas_call(
        _na_bias_kernel, grid=(NA_H,),
        in_specs=[pl.BlockSpec(memory_space=pltpu.SMEM)],
        out_specs=pl.BlockSpec((1, N_DR + 1, GW, 2 * GW), lambda h: (h, 0, 0, 0)),
        out_shape=jax.ShapeDtypeStruct((NA_H, N_DR + 1, GW, 2 * GW), F32),
        compiler_params=_params(1), name="na_bias")(rpb.reshape(-1))


def _na_kernel(*refs):
    q_ref = refs[0]
    k_refs = refs[1:1 + NA_KB]
    v_refs = refs[1 + NA_KB:1 + 2 * NA_KB]
    kc_ref, vc_ref, tz_ref, o_ref = refs[1 + 2 * NA_KB:]
    scale = HD ** -0.5
    nt = (((1,), (1,)), ((), ()))
    rows = L // GW
    g = pl.program_id(1)
    is_lat = g < rows // NA_G
    r0 = g * NA_G
    ustart = jnp.clip(r0 - NA_KH // 2, 0, rows - NA_WIN)
    lane = lax.broadcasted_iota(I32, (1, NA_WIN * GW), 1)
    masks, dds = [], []
    for a in range(NA_G):
        r = r0 + a
        lo = (jnp.clip(r - NA_KH // 2, 0, rows - NA_KH) - ustart) * GW
        hi = jnp.where(is_lat, lo + NA_KH * GW, lo)
        masks.append((lane >= lo) & (lane < hi))
        dds.append([jnp.clip(ustart + 2 * p - r + NA_KH, 0, N_DR) for p in range(NA_WIN // 2)])
    for h in range(NA_H):
        hs = slice(h * HD, (h + 1) * HD)
        qh = q_ref[0, :, hs]
        kh = jnp.concatenate([r[0, :, hs] for r in k_refs], axis=0)
        vh = jnp.concatenate([r[0, :, hs] for r in v_refs], axis=0)
        s = lax.dot_general(qh, kh, nt, preferred_element_type=F32) * scale
        parts = []
        for a in range(NA_G):
            bias = jnp.concatenate([tz_ref[h, dd] for dd in dds[a]], axis=1)
            parts.append(jnp.where(masks[a], s[a * GW:(a + 1) * GW] + bias, NEG))
        s = jnp.concatenate(parts, axis=0)
        sc = lax.dot_general(qh, kc_ref[0, :, hs], nt, preferred_element_type=F32) * scale
        m = jnp.maximum(jnp.max(s, axis=1, keepdims=True), jnp.max(sc, axis=1, keepdims=True))
        p = jnp.exp(s - m)
        pc = jnp.exp(sc - m)
        den = jnp.sum(p, axis=1, keepdims=True) + jnp.sum(pc, axis=1, keepdims=True)
        o = jnp.dot(p.astype(BF16), vh, preferred_element_type=F32)
        o = o + jnp.dot(pc.astype(BF16), vc_ref[0, :, hs], preferred_element_type=F32)
        o_ref[0, :, hs] = (o / den).astype(o_ref.dtype)


def _na(qkv, tz):
    rows = L // GW
    assert LC == NA_QB and (rows - NA_WIN) % NA_G == 0

    def win(j, part):
        return pl.BlockSpec(
            (1, NA_QB, D), lambda b, g: (b, jnp.clip(g * NA_G - NA_KH // 2, 0, rows - NA_WIN) // NA_G + j, part))

    in_specs = [pl.BlockSpec((1, NA_QB, D), lambda b, g: (b, g, 0))]
    in_specs += [win(j, 1) for j in range(NA_KB)] + [win(j, 2) for j in range(NA_KB)]
    in_specs += [pl.BlockSpec((1, LC, D), lambda b, g: (b, L // LC, 1)),
                 pl.BlockSpec((1, LC, D), lambda b, g: (b, L // LC, 2)),
                 pl.BlockSpec(tz.shape, lambda b, g: (0, 0, 0, 0))]
    return pl.pallas_call(
        _na_kernel, grid=(NB, S // NA_QB),
        in_specs=in_specs,
        out_specs=pl.BlockSpec((1, NA_QB, D), lambda b, g: (b, g, 0)),
        out_shape=jax.ShapeDtypeStruct((NB, S, D), BF16),
        compiler_params=_params(2), name="na")(*([qkv] * (3 + 2 * NA_KB)), tz)


def _filter_kernel(z_ref, w1_ref, b1_ref, w2_ref, b2_ref, w3_ref, fr_ref, dl_ref, o_ref):
    hp = lax.Precision.HIGHEST
    z = z_ref[...]
    fr = fr_ref[...]
    h = jnp.sin(fr * (jnp.dot(z, w1_ref[...], precision=hp, preferred_element_type=F32) + b1_ref[...]))
    for i in range(w2_ref.shape[0]):
        h = jnp.sin(fr * (jnp.dot(h, w2_ref[i], precision=hp, preferred_element_type=F32) + b2_ref[i]))
    h = jnp.dot(h, w3_ref[...], precision=hp, preferred_element_type=F32)
    decay = jnp.exp(-z[:, 0:1] * dl_ref[...])
    o_ref[...] = (h * jnp.concatenate([decay] * 4, axis=1)).astype(o_ref.dtype)


def _filter_embedding(n):
    t = jnp.linspace(0.0, 1.0, n, dtype=F32)[:, None]
    w = (2.0 * math.pi / n) * jnp.arange(n, dtype=F32)[:, None]
    f = jnp.linspace(1e-4, HY_BANDS - 1, HY_BANDS, dtype=F32)[None, :]
    z = jnp.concatenate([t, jnp.cos(f * w), -jnp.sin(f * w)], axis=-1)
    return jnp.pad(z, ((0, 0), (0, HD - HY_EMB)))


def _hyena_filters(n, fw1, fb1, fw2, fb2, fw3, ffreq):
    deltas = jnp.abs(jnp.linspace(math.log(HY_TARGET) / HY_SLOW_DECAY, math.log(HY_TARGET) / HY_FAST_DECAY,
                                  HY, dtype=F32))[None, :]
    w1 = jnp.pad(fw1, ((0, HD - HY_EMB), (0, 0)))
    tm = min(n, 256)
    nout = fw3.shape[1]
    full = lambda a: pl.BlockSpec(a.shape, lambda i: (0,) * a.ndim)
    args = (_filter_embedding(n), w1, fb1.reshape(1, -1), fw2, fb2.reshape(fb2.shape[0], 1, -1), fw3,
            ffreq.reshape(1, -1), deltas)
    return pl.pallas_call(
        _filter_kernel, grid=(n // tm,),
        in_specs=[pl.BlockSpec((tm, HD), lambda i: (i, 0))] + [full(a) for a in args[1:]],
        out_specs=pl.BlockSpec((tm, nout), lambda i: (i, 0)),
        out_shape=jax.ShapeDtypeStruct((n, nout), BF16),
        compiler_params=_params(1), name="hy_filter")(*args)


def _shortconv_kernel(u_ref, w_ref, b_ref, o_ref):
    u = u_ref[0]
    t = lax.broadcasted_iota(I32, (S, 1), 0)
    prev = jnp.where((t == 0) | (t == L), 0.0, pltpu.roll(u, 1, 0))
    nxt = jnp.where((t == L - 1) | (t == S - 1), 0.0, pltpu.roll(u, S - 1, 0))
    w = w_ref[...]
    o_ref[0] = b_ref[...] + prev * w[0:1] + u * w[1:2] + nxt * w[2:3]


def _shortconv(u, w, b):
    tc = 256
    nc = u.shape[2]
    return pl.pallas_call(
        _shortconv_kernel, grid=(NB, nc // tc),
        in_specs=[pl.BlockSpec((1, S, tc), lambda b_, c: (b_, 0, c)),
                  pl.BlockSpec((3, tc), lambda b_, c: (0, c)),
                  pl.BlockSpec((1, tc), lambda b_, c: (0, c))],
        out_specs=pl.BlockSpec((1, S, tc), lambda b_, c: (b_, 0, c)),
        out_shape=jax.ShapeDtypeStruct(u.shape, F32),
        compiler_params=_params(2), name="shortconv")(u, w, b.reshape(1, -1))


def _dft_tables(n):
    nn = 2 * n
    lo = 64
    f = jnp.arange(n, dtype=I32)[:, None]
    angle = lambda m: (m % nn).astype(F32) * (2.0 * math.pi / nn)
    a = angle(f * (jnp.arange(n // lo, dtype=I32) * lo)[None, :])[:, :, None]
    b = angle(f * jnp.arange(lo, dtype=I32)[None, :])[:, None, :]
    c = (jnp.cos(a) * jnp.cos(b) - jnp.sin(a) * jnp.sin(b)).reshape(n, n)
    s = -(jnp.sin(a) * jnp.cos(b) + jnp.cos(a) * jnp.sin(b)).reshape(n, n)
    i = jnp.arange(n, dtype=I32)
    alt = jnp.where(i % 2 == 0, 1.0, -1.0).astype(F32)
    fwd = jnp.concatenate([c, jnp.where(i[:, None] == 0, alt[None, :], s)], axis=0)
    inv_r = c * jnp.where(i == 0, 1.0 / nn, 2.0 / nn).astype(F32)[None, :]
    inv_i = jnp.where(i[None, :] == 0, alt[:, None] / nn, s * (2.0 / nn))
    return fwd.astype(BF16), inv_r.astype(BF16), inv_i.astype(BF16)


def _mm_kernel(a_ref, b_ref, o_ref):
    o_ref[...] = jnp.dot(a_ref[...], b_ref[...], preferred_element_type=F32).astype(o_ref.dtype)


def _mm(a, b, tm, tn, out_dtype, name):
    m, k = a.shape
    n = b.shape[1]
    return pl.pallas_call(
        _mm_kernel, grid=(m // tm, n // tn),
        in_specs=[pl.BlockSpec((tm, k), lambda i, j: (i, 0)), pl.BlockSpec((k, tn), lambda i, j: (0, j))],
        out_specs=pl.BlockSpec((tm, tn), lambda i, j: (i, j)),
        out_shape=jax.ShapeDtypeStruct((m, n), out_dtype),
        compiler_params=_params(2), name=name)(a, b)


def _dft_mul_kernel(z_ref, c_ref, s_ref, fr_ref, br_ref, fi_ref, bi_ref, yr_ref, yi_ref, zbf_ref):
    f = pl.program_id(2)

    @pl.when(f == 0)
    def _():
        zbf_ref[...] = z_ref[0].astype(BF16)

    z = zbf_ref[...]
    zr = jnp.dot(c_ref[...], z, preferred_element_type=F32)
    zi = jnp.dot(s_ref[...], z, preferred_element_type=F32)
    first = (lax.broadcasted_iota(I32, zr.shape, 0) == 0) & (f == 0)
    kr = fr_ref[...] + br_ref[...]
    ki = jnp.where(first, fi_ref[...] + bi_ref[...], fi_ref[...] - bi_ref[...])
    ii = zi * ki
    yr_ref[0] = (zr * kr - jnp.where(first, 0.0, ii)).astype(yr_ref.dtype)
    yi_ref[0] = jnp.where(first, ii, zr * ki + zi * kr).astype(yi_ref.dtype)


def _dft_mul(z, zrow, zcol, n, fwd, spec, order):
    tf = min(n, 512)
    tn = 512
    nf = n // tf
    nct = HY // tn
    zc0 = zcol * nct
    oc = order * 2 * nct

    def fspec(imag, direction):
        return pl.BlockSpec((tf, tn), lambda b, c, f: (imag * nf + f, oc + direction * nct + c))

    out = pl.BlockSpec((1, tf, tn), lambda b, c, f: (b, f, c))
    return pl.pallas_call(
        _dft_mul_kernel, grid=(NB, nct, nf),
        in_specs=[pl.BlockSpec((1, n, tn), lambda b, c, f: (b, zrow, zc0 + c)),
                  pl.BlockSpec((tf, n), lambda b, c, f: (f, 0)),
                  pl.BlockSpec((tf, n), lambda b, c, f: (nf + f, 0)),
                  fspec(0, 0), fspec(0, 1), fspec(1, 0), fspec(1, 1)],
        out_specs=[out, out],
        out_shape=[jax.ShapeDtypeStruct((NB, n, HY), BF16)] * 2,
        scratch_shapes=[pltpu.VMEM((n, tn), BF16)],
        compiler_params=_params(3), name=f"dft_mul{n}")(z, fwd, fwd, spec, spec, spec, spec)


def _idft_gate_kernel(*refs, aliased):
    if aliased:
        refs = refs[1:]
    ar_ref, ai_ref, yr_ref, yi_ref, z_ref, p_ref, bias_ref, o_ref = refs
    y = jnp.dot(ar_ref[...], yr_ref[0], preferred_element_type=F32)
    y = y + jnp.dot(ai_ref[...], yi_ref[0], preferred_element_type=F32)
    o_ref[0] = (p_ref[0] * (y + z_ref[0] * bias_ref[0])).astype(o_ref.dtype)


def _idft_gate(yr, yi, inv_r, inv_i, n, z, z_r0, z_c0, p, p_r0, p_c0, bias, order, out_rows, out_r0, out_dtype,
               into=None):
    tm = min(n, 512)
    tn = 512
    in_specs = [pl.BlockSpec((tm, n), lambda b, c, t: (t, 0)),
                pl.BlockSpec((tm, n), lambda b, c, t: (t, 0)),
                pl.BlockSpec((1, n, tn), lambda b, c, t: (b, 0, c)),
                pl.BlockSpec((1, n, tn), lambda b, c, t: (b, 0, c)),
                pl.BlockSpec((1, tm, tn), lambda b, c, t: (b, z_r0 + t, z_c0 + c)),
                pl.BlockSpec((1, tm, tn), lambda b, c, t: (b, p_r0 + t, p_c0 + c)),
                pl.BlockSpec((1, 1, tn), lambda b, c, t: (order, 0, c))]
    args = [inv_r, inv_i, yr, yi, z, p, bias]
    aliases = {}
    if into is not None:
        in_specs = [pl.BlockSpec(memory_space=pl.ANY)] + in_specs
        args = [into] + args
        aliases = {0: 0}
    return pl.pallas_call(
        functools.partial(_idft_gate_kernel, aliased=into is not None), grid=(NB, HY // tn, n // tm),
        in_specs=in_specs,
        out_specs=pl.BlockSpec((1, tm, tn), lambda b, c, t: (b, out_r0 + t, c)),
        out_shape=jax.ShapeDtypeStruct((NB, out_rows, HY), out_dtype),
        input_output_aliases=aliases,
        compiler_params=_params(3), name=f"idft_gate{n}_{order}")(*args)


def _hyena(u, conv_w, conv_b, filt, hy_bias, tables):
    sc = _shortconv(u, conv_w, conv_b)
    bias = hy_bias.reshape(2, 1, HY)
    nct = HY // 512
    out = None
    for n in (L, LC):
        zrow = 0 if n == L else L // LC
        r0 = zrow * (n // min(n, 512))
        fwd, inv_r, inv_i = tables[n]
        spec = _mm(fwd, _hyena_filters(n, *filt), min(2 * n, 512), 512, F32, f"filt_dft{n}")
        yr, yi = _dft_mul(sc, zrow, 2, n, fwd, spec, 0)
        z1 = _idft_gate(yr, yi, inv_r, inv_i, n, sc, r0, 2 * nct, sc, r0, 0, bias, 0, n, 0, F32)
        yr, yi = _dft_mul(z1, 0, 0, n, fwd, spec, 1)
        out = _idft_gate(yr, yi, inv_r, inv_i, n, z1, 0, 0, sc, r0, nct, bias, 1, S, r0, BF16, into=out)
    return out


def _resid_ln(x, y, gate, g, b):
    v = DN_ALPHA * x + gate * y
    mu = jnp.mean(v, axis=-1, keepdims=True)
    vc = v - mu
    var = jnp.mean(vc * vc, axis=-1, keepdims=True)
    return vc * lax.rsqrt(var + LN_EPS) * g + b


def _wout_ln_kernel(a1_ref, a2_ref, w_ref, x_ref, gate_ref, g_ref, b_ref, sc_ref, sh_ref, wrh_ref, wrl_ref,
                    xo_ref, hm_ref, aff_ref):
    half = a1_ref.shape[2]
    y = jnp.dot(a1_ref[0], w_ref[0:half, :], preferred_element_type=F32)
    y = y + jnp.dot(a2_ref[0], w_ref[half:2 * half, :], preferred_element_type=F32)
    xn = _resid_ln(x_ref[0], y, gate_ref[0], g_ref[0], b_ref[0])
    xo_ref[0] = xn
    hm = xn * (1.0 + sc_ref[0]) + sh_ref[0]
    hm_hi = hm.astype(BF16)
    hm_ref[0] = hm_hi
    hm_lo = (hm - hm_hi.astype(F32)).astype(BF16)
    logits = jnp.dot(hm_hi, wrh_ref[...], preferred_element_type=F32)
    logits = logits + jnp.dot(hm_lo, wrh_ref[...], preferred_element_type=F32)
    logits = logits + jnp.dot(hm_hi, wrl_ref[...], preferred_element_type=F32)
    e = jnp.exp(logits - jnp.max(logits, axis=-1, keepdims=True))
    aff_ref[0] = e / jnp.sum(e, axis=-1, keepdims=True)


def _wout_ln(a1, c1, a2, c2, w_bf, x, mod, layer, lng, lnb, w_router):
    half = D // 2
    tile = lambda: pl.BlockSpec((1, TM, D), lambda b, i: (b, i, 0))
    wr_hi = w_router.astype(BF16)
    wr_lo = (w_router - wr_hi.astype(F32)).astype(BF16)
    return pl.pallas_call(
        _wout_ln_kernel, grid=(NB, S // TM),
        in_specs=[pl.BlockSpec((1, TM, half), lambda b, i: (b, i, c1)),
                  pl.BlockSpec((1, TM, half), lambda b, i: (b, i, c2)),
                  pl.BlockSpec((D, D), lambda b, i: (0, 0)),
                  tile(), _mod_spec(layer, 2),
                  pl.BlockSpec((1, 1, D), lambda b, i: (2 * layer, 0, 0)),
                  pl.BlockSpec((1, 1, D), lambda b, i: (2 * layer, 0, 0)),
                  _mod_spec(layer, 4), _mod_spec(layer, 3),
                  pl.BlockSpec((D, NE), lambda b, i: (0, 0)),
                  pl.BlockSpec((D, NE), lambda b, i: (0, 0))],
        out_specs=[tile(), tile(), pl.BlockSpec((1, TM, NE), lambda b, i: (b, i, 0))],
        out_shape=[jax.ShapeDtypeStruct((NB, S, D), F32), jax.ShapeDtypeStruct((NB, S, D), BF16),
                   jax.ShapeDtypeStruct((NB, S, NE), F32)],
        compiler_params=_params(2), name="wout_ln")(a1, a2, w_bf, x, mod, lng, lnb, mod, mod, wr_hi, wr_lo)


def _route_kernel(a_ref, o_ref, tri_ref):
    @pl.when(pl.program_id(0) == 0)
    def _():
        rb = 256
        for r in range(L // rb):
            s_i = lax.broadcasted_iota(I32, (rb, L), 0) + r * rb
            t_i = lax.broadcasted_iota(I32, (rb, L), 1)
            tri_ref[r * rb:(r + 1) * rb, :] = jnp.where(s_i < t_i, 1.0, 0.0).astype(BF16)

    def select(a, cap, base):
        n = a.shape[1]

        def body(i, prefix):
            cand = prefix | jnp.left_shift(jnp.int32(1), 30 - i)
            cnt = jnp.sum(jnp.where(a >= lax.bitcast_convert_type(cand, F32), 1.0, 0.0), axis=1, keepdims=True)
            return jnp.where(cnt >= cap, cand, prefix)

        kth = lax.bitcast_convert_type(lax.fori_loop(0, 31, body, jnp.zeros((NE, 1), I32)), F32)
        gt = a > kth
        eq = a == kth
        n_gt = jnp.sum(jnp.where(gt, 1.0, 0.0), axis=1, keepdims=True)
        tri = tri_ref[0:n, 0:n]
        before = jnp.dot(jnp.where(eq, 1.0, 0.0).astype(BF16), tri, preferred_element_type=F32)
        sel = gt | (eq & (before < cap - n_gt))
        pos = jnp.dot(jnp.where(sel, 1.0, 0.0).astype(BF16), tri, preferred_element_type=F32)
        return jnp.where(sel, pos.astype(I32) + base, -1)

    a = a_ref[0]
    o_ref[0] = jnp.concatenate([select(a[:, :L], CAP_L, 0), select(a[:, L:], CAP_C, CAP_L)], axis=1)


def _route(aff_t):
    return pl.pallas_call(
        _route_kernel, grid=(NB,),
        in_specs=[pl.BlockSpec((1, NE, S), lambda b: (b, 0, 0))],
        out_specs=pl.BlockSpec((1, NE, S), lambda b: (b, 0, 0)),
        out_shape=jax.ShapeDtypeStruct((NB, NE, S), I32),
        scratch_shapes=[pltpu.VMEM((L, L), BF16)],
        compiler_params=_params(1), name="route")(aff_t)


def _gather_kernel(slot_ref, aff_ref, h_ref, xs_ref, g_ref):
    slot = slot_ref[0, 0]
    aff = aff_ref[0, 0]
    for r0, cap, c0, n in ((0, CAP_L, 0, L), (CAP_L, CAP_C, L, LC)):
        hit = slot[:, c0:c0 + n] == lax.broadcasted_iota(I32, (cap, n), 0) + r0
        xs = jnp.dot(jnp.where(hit, 1.0, 0.0).astype(BF16), h_ref[0, c0:c0 + n, :], preferred_element_type=F32)
        xs_ref[0, 0, r0:r0 + cap, :] = xs.astype(xs_ref.dtype)
        g = jnp.sum(jnp.where(hit, aff[:, c0:c0 + n], 0.0), axis=1, keepdims=True)
        g_ref[0, 0, r0:r0 + cap, :] = jnp.broadcast_to(g, (cap, HD))


def _gather(slot, aff_t, hm):
    row = lambda: pl.BlockSpec((1, 1, 1, S), lambda b, e: (b, e, 0, 0))
    return pl.pallas_call(
        _gather_kernel, grid=(NB, NE),
        in_specs=[row(), row(), pl.BlockSpec((1, S, D), lambda b, e: (b, 0, 0))],
        out_specs=[pl.BlockSpec((1, 1, CAP, D), lambda b, e: (b, e, 0, 0)),
                   pl.BlockSpec((1, 1, CAP, HD), lambda b, e: (b, e, 0, 0))],
        out_shape=[jax.ShapeDtypeStruct((NB, NE, CAP, D), BF16), jax.ShapeDtypeStruct((NB, NE, CAP, HD), F32)],
        compiler_params=_params(2), name="moe_gather")(
            slot.reshape(NB, NE, 1, S), aff_t.reshape(NB, NE, 1, S), hm)


def _ffn_kernel(xs_ref, g_ref, w1_ref, w3_ref, w2_ref, o_ref, acc_ref):
    f = pl.program_id(1)
    x = xs_ref[:, 0].reshape(NB * CAP, D)
    a = jnp.dot(x, w1_ref[0].astype(BF16), preferred_element_type=F32)
    u = jnp.dot(x, w3_ref[0].astype(BF16), preferred_element_type=F32)
    hmid = (a * jax.nn.sigmoid(a) * u).astype(BF16)
    part = jnp.dot(hmid, w2_ref[0].astype(BF16), preferred_element_type=F32)

    @pl.when(f == 0)
    def _():
        acc_ref[...] = part

    @pl.when(f > 0)
    def _():
        acc_ref[...] += part

    @pl.when(f == pl.num_programs(1) - 1)
    def _():
        g = g_ref[:, 0].reshape(NB * CAP, HD)[:, 0:1]
        o_ref[:, 0] = (acc_ref[...] * g).astype(o_ref.dtype).reshape(NB, CAP, D)


def _ffn(xs, gate, w1, w3, w2, layer):
    tf = 256
    blk = lambda width: pl.BlockSpec((NB, 1, CAP, width), lambda e, f: (0, e, 0, 0))
    return pl.pallas_call(
        _ffn_kernel, grid=(NE, FF // tf),
        in_specs=[blk(D), blk(HD),
                  pl.BlockSpec((None, 1, D, tf), lambda e, f: (layer, e, 0, f)),
                  pl.BlockSpec((None, 1, D, tf), lambda e, f: (layer, e, 0, f)),
                  pl.BlockSpec((None, 1, tf, D), lambda e, f: (layer, e, f, 0))],
        out_specs=blk(D),
        out_shape=jax.ShapeDtypeStruct((NB, NE, CAP, D), BF16),
        scratch_shapes=[pltpu.VMEM((NB * CAP, D), F32)],
        compiler_params=_params(2), name="moe_ffn")(xs, gate, w1, w3, w2)


def _combine_ln_kernel(*refs, last):
    if last:
        slot_ref, y_ref, x_ref, gate_ref, g_ref, b_ref, xo_ref, acc_ref = refs
    else:
        slot_ref, y_ref, x_ref, gate_ref, g_ref, b_ref, sc_ref, sh_ref, xo_ref, hn_ref, acc_ref = refs
    st = slot_ref[0]

    def scatter(r0, cap):
        want = lax.broadcasted_iota(I32, (TM, cap), 1) + r0
        acc = None
        for e in range(NE):
            hit = jnp.where(st[:, e:e + 1] == want, 1.0, 0.0).astype(BF16)
            part = jnp.dot(hit, y_ref[0, e, r0:r0 + cap, :], preferred_element_type=F32)
            acc = part if acc is None else acc + part
        acc_ref[...] = acc

    if last:
        scatter(0, CAP_L)
    else:
        is_lat = pl.program_id(1) < N_LAT_TILES
        pl.when(is_lat)(lambda: scatter(0, CAP_L))
        pl.when(jnp.logical_not(is_lat))(lambda: scatter(CAP_L, CAP_C))
    xn = _resid_ln(x_ref[0], acc_ref[...], gate_ref[0], g_ref[0], b_ref[0])
    xo_ref[0] = xn
    if not last:
        hn_ref[0] = (xn * (1.0 + sc_ref[0]) + sh_ref[0]).astype(hn_ref.dtype)


def _combine_ln(slot_t, yg, x, mod, layer, lng, lnb, last):
    rows = L if last else S
    tile = lambda: pl.BlockSpec((1, TM, D), lambda b, i: (b, i, 0))
    ln_vec = lambda: pl.BlockSpec((1, 1, D), lambda b, i: (2 * layer + 1, 0, 0))
    in_specs = [pl.BlockSpec((1, TM, NE), lambda b, i: (b, i, 0)),
                pl.BlockSpec((1, NE, CAP, D), lambda b, i: (b, 0, 0, 0), pipeline_mode=pl.Buffered(1)),
                tile(), _mod_spec(layer, 5), ln_vec(), ln_vec()]
    args = [slot_t, yg, x, mod, lng, lnb]
    out_specs = [tile()]
    out_shape = [jax.ShapeDtypeStruct((NB, rows, D), F32)]
    if not last:
        in_specs += [_mod_spec(layer + 1, 1), _mod_spec(layer + 1, 0)]
        args += [mod, mod]
        out_specs.append(tile())
        out_shape.append(jax.ShapeDtypeStruct((NB, rows, D), BF16))
    return pl.pallas_call(
        functools.partial(_combine_ln_kernel, last=last), grid=(NB, rows // TM),
        in_specs=in_specs, out_specs=out_specs, out_shape=out_shape,
        scratch_shapes=[pltpu.VMEM((TM, D), F32)],
        compiler_params=_params(2), name="moe_combine_ln")(*args)


def _moe(hm, aff, x, mod, layer, lng, lnb, w1, w3, w2, last):
    aff_t = jnp.swapaxes(aff, 1, 2)
    slot = _route(aff_t)
    xs, gate = _gather(slot, aff_t, hm)
    yg = _ffn(xs, gate, w1, w3, w2, layer)
    return _combine_ln(jnp.swapaxes(slot, 1, 2), yg, x, mod, layer, lng, lnb, last)


def kernel(x, c, ctx, c_ctx, ada_w, ada_b, ln_g, ln_b, ev_w_in, ev_w_out, hy_conv_w, hy_conv_b, hy_f_w1, hy_f_b1,
           hy_f_w2, hy_f_b2, hy_f_w3, hy_f_freq, hy_bias, swa_sink, od_w_in, od_w_out, na_rpb, moe_w_router,
           moe_w1, moe_w3, moe_w2):
    assert x.shape == (NB, L, D) and ctx.shape == (NB, LC, D)
    xa = jnp.concatenate([x, ctx], axis=1)
    cc = jnp.concatenate([c, c_ctx[None, :], jnp.zeros((8 - NB - 1, D), F32)], axis=0)
    mod = _ada(cc, ada_w, ada_b).reshape(DEPTH * 8 * 6, 1, D)
    lng = ln_g.reshape(DEPTH * 2, 1, D)
    lnb = ln_b.reshape(DEPTH * 2, 1, D)
    rope = _rope_tables()
    tables = {L: _dft_tables(L), LC: _dft_tables(LC)}
    h = _modulate(xa, mod, 0)
    for l in range(DEPTH):
        i = l // 2
        last = l == DEPTH - 1
        if l % 2 == 0:
            u = _proj(h, ev_w_in, i, 0, EV_Q0, 1024, 0, rope, F32, "proj_hy")
            q = _proj(h, ev_w_in, i, EV_Q0, SWA_H * HD, 1024, SWA_H * HD, rope, BF16, "proj_q")
            kv = _proj(h, ev_w_in, i, EV_K0, 2 * SWA_KV * HD, 512, SWA_KV * HD, rope, BF16, "proj_kv")
            att = _swa(q, kv, swa_sink[i])
            filt = (hy_f_w1[i], hy_f_b1[i], hy_f_w2[i], hy_f_b2[i], hy_f_w3[i], hy_f_freq[i])
            z = _hyena(u, hy_conv_w[i], hy_conv_b[i], filt, hy_bias[i], tables)
            xa, hm, aff = _wout_ln(z, 0, att, 0, ev_w_out[i].astype(BF16), xa, mod, l, lng, lnb, moe_w_router[l])
        else:
            qkv = _proj(h, od_w_in, i, 0, 3 * D, 1024, 0, rope, BF16, "proj_qkv")
            att = _na(qkv, _na_bias(na_rpb[i]))
            xa, hm, aff = _wout_ln(att, 0, att, 1, od_w_out[i].astype(BF16), xa, mod, l, lng, lnb,
                                   moe_w_router[l])
        out = _moe(hm, aff, xa, mod, l, lng, lnb, moe_w1, moe_w3, moe_w2, last)
        if last:
            return out[0]
        xa, h = out
```

```python
import functools
import math

import jax
import jax.numpy as jnp
from jax import lax
from jax.experimental import pallas as pl
from jax.experimental.pallas import tpu as pltpu

F32 = jnp.float32
BF16 = jnp.bfloat16
I32 = jnp.int32

D = 2048
NB = 4
L = 2048
LC = 256
S = L + LC
DEPTH = 4
GW = 64
HD = 128
HY = D // 2
HY_BANDS = 16
HY_EMB = 2 * HY_BANDS + 1
HY_HID = 64
HY_FAST_DECAY = 0.3
HY_SLOW_DECAY = 1.5
HY_TARGET = 1e-2
SWA_H = (D - HY) // HD
SWA_KV = SWA_H // 4
SWA_G = SWA_H // SWA_KV
SWA_W = 128
NA_H = D // HD
NA_KH = 8
NA_KW = 16
ROPE_BASE = 10000.0
NE = 16
CAP_L = 2 * L // NE
CAP_C = 2 * LC // NE
CAP = CAP_L + CAP_C
FF = D // 2
DN_ALPHA = (2 * DEPTH) ** 0.25
LN_EPS = 1e-5
EV_Q0 = 3 * HY
EV_K0 = EV_Q0 + SWA_H * HD
EV_IN = EV_K0 + 2 * SWA_KV * HD
NEG = -1e30

TM = 256
N_LAT_TILES = L // TM
TMW = 3 * TM
VMEM_LIMIT = 56 * 1024 * 1024


def _params(n_axes, vmem=VMEM_LIMIT):
    return pltpu.CompilerParams(dimension_semantics=("arbitrary",) * n_axes, vmem_limit_bytes=vmem)


def _mod_spec(layer, k):
    def index(b, i):
        row = jnp.where(i >= N_LAT_TILES, NB, b)
        return ((layer * 8 + row) * 6 + k, 0, 0)
    return pl.BlockSpec((1, 1, D), index)


def _ada_kernel(c_ref, w_ref, b_ref, o_ref):
    c = c_ref[...]
    s = c * jax.nn.sigmoid(c)
    o_ref[0] = jnp.dot(s.astype(BF16), w_ref[0].astype(BF16), preferred_element_type=F32) + b_ref[0]


def _ada(cc, ada_w, ada_b):
    tn = 1024
    return pl.pallas_call(
        _ada_kernel, grid=(DEPTH, 6 * D // tn),
        in_specs=[pl.BlockSpec((8, D), lambda l, j: (0, 0)),
                  pl.BlockSpec((1, D, tn), lambda l, j: (l, 0, j)),
                  pl.BlockSpec((1, 1, tn), lambda l, j: (l, 0, j))],
        out_specs=pl.BlockSpec((1, 8, tn), lambda l, j: (l, 0, j)),
        out_shape=jax.ShapeDtypeStruct((DEPTH, 8, 6 * D), F32),
        compiler_params=_params(2), name="ada")(cc, ada_w, ada_b.reshape(DEPTH, 1, 6 * D))


def _mod_kernel(x_ref, sc_ref, sh_ref, o_ref):
    o_ref[0] = (x_ref[0] * (1.0 + sc_ref[0]) + sh_ref[0]).astype(o_ref.dtype)


def _modulate(x, mod, layer):
    return pl.pallas_call(
        _mod_kernel, grid=(NB, S // TM),
        in_specs=[pl.BlockSpec((1, TM, D), lambda b, i: (b, i, 0)), _mod_spec(layer, 1), _mod_spec(layer, 0)],
        out_specs=pl.BlockSpec((1, TM, D), lambda b, i: (b, i, 0)),
        out_shape=jax.ShapeDtypeStruct((NB, S, D), BF16),
        compiler_params=_params(2), name="modulate")(x, mod, mod)


def _swap32(x):
    n = x.shape[1]
    lane = lax.broadcasted_iota(I32, x.shape, 1)
    return jnp.where((lane & 32) == 0, pltpu.roll(x, n - 32, 1), pltpu.roll(x, 32, 1))


def _proj_kernel(a_ref, w_ref, cos_ref, sin_ref, o_ref, wbf_ref, *, rope_cols):
    @pl.when((pl.program_id(1) == 0) & (pl.program_id(2) == 0))
    def _():
        wbf_ref[...] = w_ref[...].astype(BF16)

    acc = jnp.dot(a_ref[0], wbf_ref[...], preferred_element_type=F32)
    if rope_cols:
        xr = acc[:, :rope_cols]
        reps = rope_cols // HD
        cos = jnp.concatenate([cos_ref[...]] * reps, axis=1)
        sin = jnp.concatenate([sin_ref[...]] * reps, axis=1)
        xr = xr * cos + _swap32(xr) * sin
        acc = xr if rope_cols == acc.shape[1] else jnp.concatenate([xr, acc[:, rope_cols:]], axis=1)
    o_ref[0] = acc.astype(o_ref.dtype)


def _proj(h, w, layer, col0, ncols, tn, rope_cols, rope, out_dtype, name):
    assert col0 % tn == 0 and ncols % tn == 0
    off = col0 // tn
    cos_t, sin_t = rope
    return pl.pallas_call(
        functools.partial(_proj_kernel, rope_cols=rope_cols),
        grid=(ncols // tn, NB, S // TMW),
        in_specs=[pl.BlockSpec((1, TMW, D), lambda j, b, i: (b, i, 0)),
                  pl.BlockSpec((None, D, tn), lambda j, b, i: (layer, 0, j + off)),
                  pl.BlockSpec((TMW, HD), lambda j, b, i: (i, 0)),
                  pl.BlockSpec((TMW, HD), lambda j, b, i: (i, 0))],
        out_specs=pl.BlockSpec((1, TMW, tn), lambda j, b, i: (b, i, j)),
        out_shape=jax.ShapeDtypeStruct((NB, S, ncols), out_dtype),
        scratch_shapes=[pltpu.VMEM((D, tn), BF16)],
        compiler_params=_params(3), name=name)(h, w, cos_t, sin_t)


def _rope_tables():
    t = jnp.arange(S)
    nf = HD // 4
    inv = ROPE_BASE ** (-2.0 * jnp.arange(nf, dtype=F32) / (HD // 2))
    lat = (t < L)[:, None]
    cos, sin = [], []
    for pos in (t // GW, t % GW):
        ang = pos.astype(F32)[:, None] * inv[None, :]
        c = jnp.where(lat, jnp.cos(ang), 1.0)
        s = jnp.where(lat, jnp.sin(ang), 0.0)
        cos += [c, c]
        sin += [-s, s]
    return jnp.concatenate(cos, axis=1), jnp.concatenate(sin, axis=1)


def _swa_kernel(sink_ref, q_ref, k0_ref, k1_ref, k2_ref, kc_ref, v0_ref, v1_ref, v2_ref, vc_ref, mask_ref, o_ref):
    mask = mask_ref[0]
    rg = lax.broadcasted_iota(I32, (SWA_G * SWA_W, 1), 0) // SWA_W
    for hk in range(SWA_KV):
        ks = slice(hk * HD, (hk + 1) * HD)
        qs = jnp.concatenate([q_ref[0, :, (hk * SWA_G + g) * HD:(hk * SWA_G + g + 1) * HD] for g in range(SWA_G)],
                             axis=0)
        k = jnp.concatenate([k0_ref[0, :, ks], k1_ref[0, :, ks], k2_ref[0, :, ks], kc_ref[0, :, ks]], axis=0)
        v = jnp.concatenate([v0_ref[0, :, ks], v1_ref[0, :, ks], v2_ref[0, :, ks], vc_ref[0, :, ks]], axis=0)
        s = lax.dot_general(qs, k, (((1,), (1,)), ((), ())), preferred_element_type=F32) * (HD ** -0.5) + mask
        snk = jnp.zeros((SWA_G * SWA_W, 1), F32)
        for g in range(SWA_G):
            snk = jnp.where(rg == g, sink_ref[hk * SWA_G + g], snk)
        m = jnp.maximum(jnp.max(s, axis=1, keepdims=True), snk)
        p = jnp.exp(s - m)
        den = jnp.sum(p, axis=1, keepdims=True) + jnp.exp(snk - m)
        o = jnp.dot(p.astype(BF16), v, preferred_element_type=F32) / den
        for g in range(SWA_G):
            o_ref[0, :, (hk * SWA_G + g) * HD:(hk * SWA_G + g + 1) * HD] = \
                o[g * SWA_W:(g + 1) * SWA_W].astype(o_ref.dtype)


def _swa_masks():
    r = jnp.arange(SWA_W)[:, None]
    c = jnp.arange(3 * SWA_W + LC)[None, :]
    band = (c - r >= 0) & (c - r <= 2 * SWA_W)
    is_ctx = c >= 3 * SWA_W
    variants = [band & (c >= SWA_W), band, band & (c < 2 * SWA_W), jnp.zeros_like(band)]
    m = jnp.stack([jnp.where(is_ctx | v, 0.0, NEG).astype(F32) for v in variants])
    return jnp.tile(m, (1, SWA_G, 1))


def _swa(q, kv, sink, masks):
    nb = L // SWA_W
    kvw = SWA_KV * HD

    def band(j, part):
        return pl.BlockSpec((1, SWA_W, kvw), lambda b, n: (b, jnp.clip(n + j - 1, 0, nb - 1), part))

    def ctx(part):
        return pl.BlockSpec((1, LC, kvw), lambda b, n: (b, L // LC, part))

    def variant(b, n):
        return (jnp.where(n == 0, 0, jnp.where(n < nb - 1, 1, jnp.where(n == nb - 1, 2, 3))), 0, 0)

    qspec = pl.BlockSpec((1, SWA_W, SWA_H * HD), lambda b, n: (b, n, 0))
    return pl.pallas_call(
        _swa_kernel, grid=(NB, S // SWA_W),
        in_specs=[pl.BlockSpec(memory_space=pltpu.SMEM), qspec,
                  band(0, 0), band(1, 0), band(2, 0), ctx(0), band(0, 1), band(1, 1), band(2, 1), ctx(1),
                  pl.BlockSpec((1,) + masks.shape[1:], variant)],
        out_specs=qspec,
        out_shape=jax.ShapeDtypeStruct((NB, S, SWA_H * HD), BF16),
        compiler_params=_params(2), name="swa")(sink, q, kv, kv, kv, kv, kv, kv, kv, kv, masks)


N_DR = 2 * NA_KH - 1
N_DC = 2 * NA_KW - 1
NA_G = 4
NA_WIN = NA_G + NA_KH
NA_QB = NA_G * GW
NA_KB = NA_WIN * GW // NA_QB


def _na_bias_kernel(rpb_ref, o_ref):
    h = pl.program_id(0)
    shape = (GW, 2 * GW)
    lane = lax.broadcasted_iota(I32, shape, 1)
    qc = lax.broadcasted_iota(I32, shape, 0)
    kc = lane & (GW - 1)
    diff = kc - qc + (NA_KW - 1)
    cs = jnp.clip(qc - NA_KW // 2, 0, GW - NA_KW)
    ok = (kc >= cs) & (kc < cs + NA_KW)
    right = lax.broadcasted_iota(I32, (1, 2 * GW), 1) >= GW
    for dd in range(N_DR + 1):
        acc = jnp.zeros(shape, F32)
        for m in range(N_DC):
            lo = rpb_ref[(h * N_DR + dd - 1) * N_DC + m] if dd >= 1 else 0.0
            hi = rpb_ref[(h * N_DR + dd) * N_DC + m] if dd < N_DR else 0.0
            acc = jnp.where(diff == m, jnp.where(right, hi, lo), acc)
        valid = ok & (right if dd == 0 else (~right if dd == N_DR else True))
        o_ref[0, dd] = jnp.where(valid, acc, NEG)


def _na_bias(rpb):
    return pl.pallas_call(
        _na_bias_kernel, grid=(NA_H,),
        in_specs=[pl.BlockSpec(memory_space=pltpu.SMEM)],
        out_specs=pl.BlockSpec((1, N_DR + 1, GW, 2 * GW), lambda h: (h, 0, 0, 0)),
        out_shape=jax.ShapeDtypeStruct((NA_H, N_DR + 1, GW, 2 * GW), F32),
        compiler_params=_params(1), name="na_bias")(rpb.reshape(-1))


def _na_kernel(*refs):
    q_ref = refs[0]
    k_refs = refs[1:1 + NA_KB]
    v_refs = refs[1 + NA_KB:1 + 2 * NA_KB]
    kc_ref, vc_ref, tz_ref, o_ref = refs[1 + 2 * NA_KB:]
    scale = HD ** -0.5
    nt = (((1,), (1,)), ((), ()))
    rows = L // GW
    g = pl.program_id(1)
    is_lat = g < rows // NA_G
    r0 = g * NA_G
    ustart = jnp.clip(r0 - NA_KH // 2, 0, rows - NA_WIN)
    lane = lax.broadcasted_iota(I32, (1, NA_WIN * GW), 1)
    masks, dds = [], []
    for a in range(NA_G):
        r = r0 + a
        lo = (jnp.clip(r - NA_KH // 2, 0, rows - NA_KH) - ustart) * GW
        hi = jnp.where(is_lat, lo + NA_KH * GW, lo)
        masks.append((lane >= lo) & (lane < hi))
        dds.append([jnp.clip(ustart + 2 * p - r + NA_KH, 0, N_DR) for p in range(NA_WIN // 2)])
    for h in range(NA_H):
        hs = slice(h * HD, (h + 1) * HD)
        qh = q_ref[0, :, hs]
        kh = jnp.concatenate([r[0, :, hs] for r in k_refs], axis=0)
        vh = jnp.concatenate([r[0, :, hs] for r in v_refs], axis=0)
        s = lax.dot_general(qh, kh, nt, preferred_element_type=F32) * scale
        parts = []
        for a in range(NA_G):
            bias = jnp.concatenate([tz_ref[h, dd] for dd in dds[a]], axis=1)
            parts.append(jnp.where(masks[a], s[a * GW:(a + 1) * GW] + bias, NEG))
        s = jnp.concatenate(parts, axis=0)
        sc = lax.dot_general(qh, kc_ref[0, :, hs], nt, preferred_element_type=F32) * scale
        m = jnp.maximum(jnp.max(s, axis=1, keepdims=True), jnp.max(sc, axis=1, keepdims=True))
        p = jnp.exp(s - m)
        pc = jnp.exp(sc - m)
        den = jnp.sum(p, axis=1, keepdims=True) + jnp.sum(pc, axis=1, keepdims=True)
        o = jnp.dot(p.astype(BF16), vh, preferred_element_type=F32)
        o = o + jnp.dot(pc.astype(BF16), vc_ref[0, :, hs], preferred_element_type=F32)
        o_ref[0, :, hs] = (o / den).astype(o_ref.dtype)


def _na(qkv, tz):
    rows = L // GW
    assert LC == NA_QB and (rows - NA_WIN) % NA_G == 0

    def win(j, part):
        return pl.BlockSpec(
            (1, NA_QB, D), lambda b, g: (b, jnp.clip(g * NA_G - NA_KH // 2, 0, rows - NA_WIN) // NA_G + j, part))

    in_specs = [pl.BlockSpec((1, NA_QB, D), lambda b, g: (b, g, 0))]
    in_specs += [win(j, 1) for j in range(NA_KB)] + [win(j, 2) for j in range(NA_KB)]
    in_specs += [pl.BlockSpec((1, LC, D), lambda b, g: (b, L // LC, 1)),
                 pl.BlockSpec((1, LC, D), lambda b, g: (b, L // LC, 2)),
                 pl.BlockSpec(tz.shape, lambda b, g: (0, 0, 0, 0))]
    return pl.pallas_call(
        _na_kernel, grid=(NB, S // NA_QB),
        in_specs=in_specs,
        out_specs=pl.BlockSpec((1, NA_QB, D), lambda b, g: (b, g, 0)),
        out_shape=jax.ShapeDtypeStruct((NB, S, D), BF16),
        compiler_params=_params(2), name="na")(*([qkv] * (3 + 2 * NA_KB)), tz)


def _filter_kernel(z_ref, w1_ref, b1_ref, w2_ref, b2_ref, w3_ref, fr_ref, dl_ref, o_ref):
    hp = lax.Precision.HIGHEST
    z = z_ref[...]
    fr = fr_ref[...]
    h = jnp.sin(fr * (jnp.dot(z, w1_ref[...], precision=hp, preferred_element_type=F32) + b1_ref[...]))
    for i in range(w2_ref.shape[0]):
        h = jnp.sin(fr * (jnp.dot(h, w2_ref[i], precision=hp, preferred_element_type=F32) + b2_ref[i]))
    h = jnp.dot(h, w3_ref[...], precision=hp, preferred_element_type=F32)
    decay = jnp.exp(-z[:, 0:1] * dl_ref[...])
    parts = []
    for order in range(2):
        fwd = h[:, (2 * order) * HY:(2 * order + 1) * HY] * decay
        bwd = h[:, (2 * order + 1) * HY:(2 * order + 2) * HY] * decay
        parts += [fwd + bwd, fwd - bwd]
    o_ref[...] = jnp.concatenate(parts, axis=1).astype(o_ref.dtype)


def _filter_embedding(n):
    t = jnp.linspace(0.0, 1.0, n, dtype=F32)[:, None]
    w = (2.0 * math.pi / n) * jnp.arange(n, dtype=F32)[:, None]
    f = jnp.linspace(1e-4, HY_BANDS - 1, HY_BANDS, dtype=F32)[None, :]
    z = jnp.concatenate([t, jnp.cos(f * w), -jnp.sin(f * w)], axis=-1)
    return jnp.pad(z, ((0, 0), (0, HD - HY_EMB)))


def _hyena_filters(n, fw1, fb1, fw2, fb2, fw3, ffreq):
    deltas = jnp.abs(jnp.linspace(math.log(HY_TARGET) / HY_SLOW_DECAY, math.log(HY_TARGET) / HY_FAST_DECAY,
                                  HY, dtype=F32))[None, :]
    w1 = jnp.pad(fw1, ((0, HD - HY_EMB), (0, 0)))
    tm = min(n, 256)
    nout = fw3.shape[1]
    full = lambda a: pl.BlockSpec(a.shape, lambda i: (0,) * a.ndim)
    args = (_filter_embedding(n), w1, fb1.reshape(1, -1), fw2, fb2.reshape(fb2.shape[0], 1, -1), fw3,
            ffreq.reshape(1, -1), deltas)
    return pl.pallas_call(
        _filter_kernel, grid=(n // tm,),
        in_specs=[pl.BlockSpec((tm, HD), lambda i: (i, 0))] + [full(a) for a in args[1:]],
        out_specs=pl.BlockSpec((tm, nout), lambda i: (i, 0)),
        out_shape=jax.ShapeDtypeStruct((n, nout), BF16),
        compiler_params=_params(1), name="hy_filter")(*args)


def _shortconv_kernel(u_ref, w_ref, b_ref, o_ref):
    u = u_ref[0]
    t = lax.broadcasted_iota(I32, (S, 1), 0)
    prev = jnp.where((t == 0) | (t == L), 0.0, pltpu.roll(u, 1, 0))
    nxt = jnp.where((t == L - 1) | (t == S - 1), 0.0, pltpu.roll(u, S - 1, 0))
    w = w_ref[...]
    o_ref[0] = b_ref[...] + prev * w[0:1] + u * w[1:2] + nxt * w[2:3]


def _shortconv(u, w, b):
    tc = 256
    nc = u.shape[2]
    return pl.pallas_call(
        _shortconv_kernel, grid=(NB, nc // tc),
        in_specs=[pl.BlockSpec((1, S, tc), lambda b_, c: (b_, 0, c)),
                  pl.BlockSpec((3, tc), lambda b_, c: (0, c)),
                  pl.BlockSpec((1, tc), lambda b_, c: (0, c))],
        out_specs=pl.BlockSpec((1, S, tc), lambda b_, c: (b_, 0, c)),
        out_shape=jax.ShapeDtypeStruct(u.shape, F32),
        compiler_params=_params(2), name="shortconv")(u, w, b.reshape(1, -1))


def _dft_tables(n):
    nn = 2 * n
    lo = 64
    f = jnp.arange(n, dtype=I32)[:, None]
    angle = lambda m: (m % nn).astype(F32) * (2.0 * math.pi / nn)
    a = angle(f * (jnp.arange(n // lo, dtype=I32) * lo)[None, :])[:, :, None]
    b = angle(f * jnp.arange(lo, dtype=I32)[None, :])[:, None, :]
    c = (jnp.cos(a) * jnp.cos(b) - jnp.sin(a) * jnp.sin(b)).reshape(n, n)
    s = -(jnp.sin(a) * jnp.cos(b) + jnp.cos(a) * jnp.sin(b)).reshape(n, n)
    i = jnp.arange(n, dtype=I32)
    alt = jnp.where(i % 2 == 0, 1.0, -1.0).astype(F32)
    fwd = jnp.concatenate([c, jnp.where(i[:, None] == 0, alt[None, :], s)], axis=0)
    inv_r = c * jnp.where(i == 0, 1.0 / nn, 2.0 / nn).astype(F32)[None, :]
    inv_i = jnp.where(i[None, :] == 0, alt[:, None] / nn, s * (2.0 / nn))
    return fwd.astype(BF16), inv_r.astype(BF16), inv_i.astype(BF16)


HY_TN = 512
HY_NCT = HY // HY_TN


def _filt_dft_kernel(c_ref, s_ref, hs_ref, hd_ref, kr_ref, ki_ref):
    kr_ref[0] = jnp.dot(c_ref[...], hs_ref[...], preferred_element_type=F32)
    ki_ref[0] = jnp.dot(s_ref[...], hd_ref[...], preferred_element_type=F32)

    @pl.when(pl.program_id(0) == 0)
    def _():
        nyq = jnp.dot(s_ref[0:8, :], hs_ref[...], preferred_element_type=F32)
        first = lax.broadcasted_iota(I32, nyq.shape, 0) == 0
        ki_ref[0, 0:8, :] = jnp.where(first, nyq, ki_ref[0, 0:8, :])


def _filt_dft(fwd, hfilt, n):
    tf = min(n, 1024)
    nf = n // tf
    out = pl.BlockSpec((1, tf, HY_TN), lambda f, c: (c // HY_NCT, f, c % HY_NCT))
    return pl.pallas_call(
        _filt_dft_kernel, grid=(nf, 2 * HY_NCT),
        in_specs=[pl.BlockSpec((tf, n), lambda f, c: (f, 0)),
                  pl.BlockSpec((tf, n), lambda f, c: (nf + f, 0)),
                  pl.BlockSpec((n, HY_TN), lambda f, c: (0, (c // HY_NCT) * 2 * HY_NCT + c % HY_NCT)),
                  pl.BlockSpec((n, HY_TN), lambda f, c: (0, (c // HY_NCT) * 2 * HY_NCT + HY_NCT + c % HY_NCT))],
        out_specs=[out, out],
        out_shape=[jax.ShapeDtypeStruct((2, n, HY), F32)] * 2,
        compiler_params=_params(2), name=f"filt_dft{n}")(fwd, fwd, hfilt, hfilt)


def _dft_mul_kernel(z_ref, c_ref, s_ref, kr_ref, ki_ref, yr_ref, yi_ref):
    z = z_ref[0].astype(BF16)
    zr = jnp.dot(c_ref[...], z, preferred_element_type=F32)
    zi = jnp.dot(s_ref[...], z, preferred_element_type=F32)
    kr = kr_ref[...]
    ki = ki_ref[...]
    first = (lax.broadcasted_iota(I32, zr.shape, 0) == 0) & (pl.program_id(0) == 0)
    ii = zi * ki
    yr_ref[0] = (zr * kr - jnp.where(first, 0.0, ii)).astype(yr_ref.dtype)
    yi_ref[0] = jnp.where(first, ii, zr * ki + zi * kr).astype(yi_ref.dtype)


def _dft_mul(z, zrow, zc0, n, fwd, kr, ki, order):
    tf = min(n, 1024)
    nf = n // tf
    out = pl.BlockSpec((1, tf, HY_TN), lambda f, b, c: (b, f, c))
    kspec = lambda: pl.BlockSpec((None, tf, HY_TN), lambda f, b, c: (order, f, c))
    return pl.pallas_call(
        _dft_mul_kernel, grid=(nf, NB, HY_NCT),
        in_specs=[pl.BlockSpec((1, n, HY_TN), lambda f, b, c: (b, zrow, zc0 + c)),
                  pl.BlockSpec((tf, n), lambda f, b, c: (f, 0)),
                  pl.BlockSpec((tf, n), lambda f, b, c: (nf + f, 0)),
                  kspec(), kspec()],
        out_specs=[out, out],
        out_shape=[jax.ShapeDtypeStruct((NB, n, HY), BF16)] * 2,
        compiler_params=_params(3), name=f"dft_mul{n}")(z, fwd, fwd, kr, ki)


def _idft_gate_kernel(ar_ref, ai_ref, yr_ref, yi_ref, z_ref, p_ref, bias_ref, *o_refs):
    y = jnp.dot(ar_ref[...], yr_ref[0], preferred_element_type=F32)
    y = y + jnp.dot(ai_ref[...], yi_ref[0], preferred_element_type=F32)
    out = p_ref[0] * (y + z_ref[0] * bias_ref[0])
    for o_ref in o_refs:
        o_ref[0] = out.astype(o_ref.dtype)


def _idft_gate(yr, yi, inv_r, inv_i, n, z, z_r0, z_c0, p, p_r0, p_c0, bias, order, out_dtype):
    tm = min(n, 1024)
    return pl.pallas_call(
        _idft_gate_kernel, grid=(n // tm, NB, HY_NCT),
        in_specs=[pl.BlockSpec((tm, n), lambda t, b, c: (t, 0)),
                  pl.BlockSpec((tm, n), lambda t, b, c: (t, 0)),
                  pl.BlockSpec((1, n, HY_TN), lambda t, b, c: (b, 0, c)),
                  pl.BlockSpec((1, n, HY_TN), lambda t, b, c: (b, 0, c)),
                  pl.BlockSpec((1, tm, HY_TN), lambda t, b, c: (b, z_r0 + t, z_c0 + c)),
                  pl.BlockSpec((1, tm, HY_TN), lambda t, b, c: (b, p_r0 + t, p_c0 + c)),
                  pl.BlockSpec((1, 1, HY_TN), lambda t, b, c: (order, 0, c))],
        out_specs=pl.BlockSpec((1, tm, HY_TN), lambda t, b, c: (b, t, c)),
        out_shape=jax.ShapeDtypeStruct((NB, n, HY), out_dtype),
        compiler_params=_params(3), name=f"idft_gate{n}_{order}")(inv_r, inv_i, yr, yi, z, p, bias)


def _hyena(u, conv_w, conv_b, filt, hy_bias, tables):
    sc = _shortconv(u, conv_w, conv_b)
    bias = hy_bias.reshape(2, 1, HY)
    outs = []
    for n in (L, LC):
        zrow = 0 if n == L else L // LC
        r0 = zrow * (n // min(n, 1024))
        fwd, inv_r, inv_i = tables[n]
        kr, ki = _filt_dft(fwd, _hyena_filters(n, *filt), n)
        yr, yi = _dft_mul(sc, zrow, 2 * HY_NCT, n, fwd, kr, ki, 0)
        z1 = _idft_gate(yr, yi, inv_r, inv_i, n, sc, r0, 2 * HY_NCT, sc, r0, 0, bias, 0, F32)
        yr, yi = _dft_mul(z1, 0, 0, n, fwd, kr, ki, 1)
        outs.append(_idft_gate(yr, yi, inv_r, inv_i, n, z1, 0, 0, sc, r0, HY_NCT, bias, 1, BF16))
    return outs


def _resid_ln(x, y, gate, g, b):
    v = DN_ALPHA * x + gate * y
    mu = jnp.mean(v, axis=-1, keepdims=True)
    vc = v - mu
    var = jnp.mean(vc * vc, axis=-1, keepdims=True)
    return vc * lax.rsqrt(var + LN_EPS) * g + b


def _wout_ln_kernel(a1a_ref, a1b_ref, a1c_ref, a1x_ref, a2_ref, w_ref, x_ref, gl_ref, gc_ref, g_ref, b_ref,
                    scl_ref, scc_ref, shl_ref, shc_ref, wrh_ref, wrl_ref, xo_ref, hm_ref, aff_ref):
    half = a2_ref.shape[2]
    last_tile = pl.program_id(1) == pl.num_programs(1) - 1
    for s, a1_ref in enumerate((a1a_ref, a1b_ref, a1c_ref)):
        rows = slice(s * TM, (s + 1) * TM)
        may_be_ctx = s == TMW // TM - 1
        if may_be_ctx:
            a1 = jnp.where(last_tile, a1x_ref[0], a1_ref[0])
            pick = lambda lat_ref, ctx_ref: jnp.where(last_tile, ctx_ref[0], lat_ref[0])
        else:
            a1 = a1_ref[0]
            pick = lambda lat_ref, ctx_ref: lat_ref[0]
        y = jnp.dot(a1, w_ref[0:half, :], preferred_element_type=F32)
        y = y + jnp.dot(a2_ref[0, rows, :], w_ref[half:2 * half, :], preferred_element_type=F32)
        xn = _resid_ln(x_ref[0, rows, :], y, pick(gl_ref, gc_ref), g_ref[0], b_ref[0])
        xo_ref[0, rows, :] = xn
        hm = xn * (1.0 + pick(scl_ref, scc_ref)) + pick(shl_ref, shc_ref)
        hm_hi = hm.astype(BF16)
        hm_ref[0, rows, :] = hm_hi
        hm_lo = (hm - hm_hi.astype(F32)).astype(BF16)
        logits = jnp.dot(hm_hi, wrh_ref[...], preferred_element_type=F32)
        logits = logits + jnp.dot(hm_lo, wrh_ref[...], preferred_element_type=F32)
        logits = logits + jnp.dot(hm_hi, wrl_ref[...], preferred_element_type=F32)
        e = jnp.exp(logits - jnp.max(logits, axis=-1, keepdims=True))
        aff_ref[0, rows, :] = e / jnp.sum(e, axis=-1, keepdims=True)


def _wout_ln(a1_lat, a1_ctx, a1_ctx_blk, c1, a2, c2, w_bf, x, mod, layer, lng, lnb, w_router):
    half = D // 2
    sub = TMW // TM
    n_lat_blk = a1_lat.shape[1] // TM
    tile = lambda: pl.BlockSpec((1, TMW, D), lambda b, i: (b, i, 0))
    vec = lambda idx: pl.BlockSpec((1, 1, D), idx)
    lat = lambda k: vec(lambda b, i: ((layer * 8 + b) * 6 + k, 0, 0))
    ctx = lambda k: vec(lambda b, i: ((layer * 8 + NB) * 6 + k, 0, 0))
    a1_blk = lambda s: pl.BlockSpec((1, TM, half), lambda b, i: (b, jnp.minimum(i * sub + s, n_lat_blk - 1), c1))
    wr_hi = w_router.astype(BF16)
    wr_lo = (w_router - wr_hi.astype(F32)).astype(BF16)
    return pl.pallas_call(
        _wout_ln_kernel, grid=(NB, S // TMW),
        in_specs=[a1_blk(0), a1_blk(1), a1_blk(2),
                  pl.BlockSpec((1, TM, half), lambda b, i: (b, a1_ctx_blk, c1)),
                  pl.BlockSpec((1, TMW, half), lambda b, i: (b, i, c2)),
                  pl.BlockSpec((D, D), lambda b, i: (0, 0), pipeline_mode=pl.Buffered(1)),
                  tile(), lat(2), ctx(2),
                  vec(lambda b, i: (2 * layer, 0, 0)), vec(lambda b, i: (2 * layer, 0, 0)),
                  lat(4), ctx(4), lat(3), ctx(3),
                  pl.BlockSpec((D, NE), lambda b, i: (0, 0)),
                  pl.BlockSpec((D, NE), lambda b, i: (0, 0))],
        out_specs=[tile(), tile(), pl.BlockSpec((1, TMW, NE), lambda b, i: (b, i, 0))],
        out_shape=[jax.ShapeDtypeStruct((NB, S, D), F32), jax.ShapeDtypeStruct((NB, S, D), BF16),
                   jax.ShapeDtypeStruct((NB, S, NE), F32)],
        compiler_params=_params(2), name="wout_ln")(
            a1_lat, a1_lat, a1_lat, a1_ctx, a2, w_bf, x, mod, mod, lng, lnb, mod, mod, mod, mod, wr_hi, wr_lo)


def _route_kernel(a_ref, o_ref, tri_ref):
    @pl.when(pl.program_id(0) == 0)
    def _():
        rb = 256
        for r in range(L // rb):
            s_i = lax.broadcasted_iota(I32, (rb, L), 0) + r * rb
            t_i = lax.broadcasted_iota(I32, (rb, L), 1)
            tri_ref[r * rb:(r + 1) * rb, :] = jnp.where(s_i < t_i, 1.0, 0.0).astype(BF16)

    def select(a, cap, base):
        n = a.shape[1]

        def body(i, prefix):
            cand = prefix | jnp.left_shift(jnp.int32(1), 30 - i)
            cnt = jnp.sum(jnp.where(a >= lax.bitcast_convert_type(cand, F32), 1.0, 0.0), axis=1, keepdims=True)
            return jnp.where(cnt >= cap, cand, prefix)

        kth = lax.bitcast_convert_type(lax.fori_loop(0, 31, body, jnp.zeros((NE, 1), I32)), F32)
        gt = a > kth
        eq = a == kth
        n_gt = jnp.sum(jnp.where(gt, 1.0, 0.0), axis=1, keepdims=True)
        tri = tri_ref[0:n, 0:n]
        before = jnp.dot(jnp.where(eq, 1.0, 0.0).astype(BF16), tri, preferred_element_type=F32)
        sel = gt | (eq & (before < cap - n_gt))
        pos = jnp.dot(jnp.where(sel, 1.0, 0.0).astype(BF16), tri, preferred_element_type=F32)
        return jnp.where(sel, pos.astype(I32) + base, -1)

    a = a_ref[0]
    o_ref[0] = jnp.concatenate([select(a[:, :L], CAP_L, 0), select(a[:, L:], CAP_C, CAP_L)], axis=1)


def _route(aff_t):
    return pl.pallas_call(
        _route_kernel, grid=(NB,),
        in_specs=[pl.BlockSpec((1, NE, S), lambda b: (b, 0, 0))],
        out_specs=pl.BlockSpec((1, NE, S), lambda b: (b, 0, 0)),
        out_shape=jax.ShapeDtypeStruct((NB, NE, S), I32),
        scratch_shapes=[pltpu.VMEM((L, L), BF16)],
        compiler_params=_params(1), name="route")(aff_t)


def _gather_kernel(slot_ref, aff_ref, h_ref, xs_ref, g_ref):
    slot = slot_ref[0, 0]
    aff = aff_ref[0, 0]
    for r0, cap, c0, n in ((0, CAP_L, 0, L), (CAP_L, CAP_C, L, LC)):
        hit = slot[:, c0:c0 + n] == lax.broadcasted_iota(I32, (cap, n), 0) + r0
        xs = jnp.dot(jnp.where(hit, 1.0, 0.0).astype(BF16), h_ref[0, c0:c0 + n, :], preferred_element_type=F32)
        xs_ref[0, 0, r0:r0 + cap, :] = xs.astype(xs_ref.dtype)
        g = jnp.sum(jnp.where(hit, aff[:, c0:c0 + n], 0.0), axis=1, keepdims=True)
        g_ref[0, 0, r0:r0 + cap, :] = jnp.broadcast_to(g, (cap, HD))


def _gather(slot, aff_t, hm):
    row = lambda: pl.BlockSpec((1, 1, 1, S), lambda b, e: (b, e, 0, 0))
    return pl.pallas_call(
        _gather_kernel, grid=(NB, NE),
        in_specs=[row(), row(), pl.BlockSpec((1, S, D), lambda b, e: (b, 0, 0))],
        out_specs=[pl.BlockSpec((1, 1, CAP, D), lambda b, e: (b, e, 0, 0)),
                   pl.BlockSpec((1, 1, CAP, HD), lambda b, e: (b, e, 0, 0))],
        out_shape=[jax.ShapeDtypeStruct((NB, NE, CAP, D), BF16), jax.ShapeDtypeStruct((NB, NE, CAP, HD), F32)],
        compiler_params=_params(2), name="moe_gather")(
            slot.reshape(NB, NE, 1, S), aff_t.reshape(NB, NE, 1, S), hm)


def _ffn_kernel(xs_ref, g_ref, w1_ref, w3_ref, w2_ref, o_ref, acc_ref):
    f = pl.program_id(1)
    x = xs_ref[:, 0].reshape(NB * CAP, D)
    a = jnp.dot(x, w1_ref[0].astype(BF16), preferred_element_type=F32)
    u = jnp.dot(x, w3_ref[0].astype(BF16), preferred_element_type=F32)
    hmid = (a * jax.nn.sigmoid(a) * u).astype(BF16)
    part = jnp.dot(hmid, w2_ref[0].astype(BF16), preferred_element_type=F32)

    @pl.when(f == 0)
    def _():
        acc_ref[...] = part

    @pl.when(f > 0)
    def _():
        acc_ref[...] += part

    @pl.when(f == pl.num_programs(1) - 1)
    def _():
        g = g_ref[:, 0].reshape(NB * CAP, HD)[:, 0:1]
        o_ref[:, 0] = (acc_ref[...] * g).astype(o_ref.dtype).reshape(NB, CAP, D)


def _ffn(xs, gate, w1, w3, w2, layer):
    tf = 256
    blk = lambda width: pl.BlockSpec((NB, 1, CAP, width), lambda e, f: (0, e, 0, 0))
    return pl.pallas_call(
        _ffn_kernel, grid=(NE, FF // tf),
        in_specs=[blk(D), blk(HD),
                  pl.BlockSpec((None, 1, D, tf), lambda e, f: (layer, e, 0, f)),
                  pl.BlockSpec((None, 1, D, tf), lambda e, f: (layer, e, 0, f)),
                  pl.BlockSpec((None, 1, tf, D), lambda e, f: (layer, e, f, 0))],
        out_specs=blk(D),
        out_shape=jax.ShapeDtypeStruct((NB, NE, CAP, D), BF16),
        scratch_shapes=[pltpu.VMEM((NB * CAP, D), F32)],
        compiler_params=_params(2), name="moe_ffn")(xs, gate, w1, w3, w2)


def _combine_ln_kernel(*refs, last):
    if last:
        slot_ref, y_ref, x_ref, gate_ref, g_ref, b_ref, xo_ref, acc_ref = refs
    else:
        slot_ref, y_ref, x_ref, gate_ref, g_ref, b_ref, sc_ref, sh_ref, xo_ref, hn_ref, acc_ref = refs
    st = slot_ref[0]

    def scatter(r0, cap):
        want = lax.broadcasted_iota(I32, (TM, cap), 1) + r0
        acc = None
        for e in range(NE):
            hit = jnp.where(st[:, e:e + 1] == want, 1.0, 0.0).astype(BF16)
            part = jnp.dot(hit, y_ref[0, e, r0:r0 + cap, :], preferred_element_type=F32)
            acc = part if acc is None else acc + part
        acc_ref[...] = acc

    if last:
        scatter(0, CAP_L)
    else:
        is_lat = pl.program_id(1) < N_LAT_TILES
        pl.when(is_lat)(lambda: scatter(0, CAP_L))
        pl.when(jnp.logical_not(is_lat))(lambda: scatter(CAP_L, CAP_C))
    xn = _resid_ln(x_ref[0], acc_ref[...], gate_ref[0], g_ref[0], b_ref[0])
    xo_ref[0] = xn
    if not last:
        hn_ref[0] = (xn * (1.0 + sc_ref[0]) + sh_ref[0]).astype(hn_ref.dtype)


def _combine_ln(slot_t, yg, x, mod, layer, lng, lnb, last):
    rows = L if last else S
    tile = lambda: pl.BlockSpec((1, TM, D), lambda b, i: (b, i, 0))
    ln_vec = lambda: pl.BlockSpec((1, 1, D), lambda b, i: (2 * layer + 1, 0, 0))
    in_specs = [pl.BlockSpec((1, TM, NE), lambda b, i: (b, i, 0)),
                pl.BlockSpec((1, NE, CAP, D), lambda b, i: (b, 0, 0, 0), pipeline_mode=pl.Buffered(1)),
                tile(), _mod_spec(layer, 5), ln_vec(), ln_vec()]
    args = [slot_t, yg, x, mod, lng, lnb]
    out_specs = [tile()]
    out_shape = [jax.ShapeDtypeStruct((NB, rows, D), F32)]
    if not last:
        in_specs += [_mod_spec(layer + 1, 1), _mod_spec(layer + 1, 0)]
        args += [mod, mod]
        out_specs.append(tile())
        out_shape.append(jax.ShapeDtypeStruct((NB, rows, D), BF16))
    return pl.pallas_call(
        functools.partial(_combine_ln_kernel, last=last), grid=(NB, rows // TM),
        in_specs=in_specs, out_specs=out_specs, out_shape=out_shape,
        scratch_shapes=[pltpu.VMEM((TM, D), F32)],
        compiler_params=_params(2), name="moe_combine_ln")(*args)


def _moe(hm, aff, x, mod, layer, lng, lnb, w1, w3, w2, last):
    aff_t = jnp.swapaxes(aff, 1, 2)
    slot = _route(aff_t)
    xs, gate = _gather(slot, aff_t, hm)
    yg = _ffn(xs, gate, w1, w3, w2, layer)
    return _combine_ln(jnp.swapaxes(slot, 1, 2), yg, x, mod, layer, lng, lnb, last)


def kernel(x, c, ctx, c_ctx, ada_w, ada_b, ln_g, ln_b, ev_w_in, ev_w_out, hy_conv_w, hy_conv_b, hy_f_w1, hy_f_b1,
           hy_f_w2, hy_f_b2, hy_f_w3, hy_f_freq, hy_bias, swa_sink, od_w_in, od_w_out, na_rpb, moe_w_router,
           moe_w1, moe_w3, moe_w2):
    assert x.shape == (NB, L, D) and ctx.shape == (NB, LC, D)
    xa = jnp.concatenate([x, ctx], axis=1)
    cc = jnp.concatenate([c, c_ctx[None, :], jnp.zeros((8 - NB - 1, D), F32)], axis=0)
    mod = _ada(cc, ada_w, ada_b).reshape(DEPTH * 8 * 6, 1, D)
    lng = ln_g.reshape(DEPTH * 2, 1, D)
    lnb = ln_b.reshape(DEPTH * 2, 1, D)
    rope = _rope_tables()
    tables = {L: _dft_tables(L), LC: _dft_tables(LC)}
    swa_masks = _swa_masks()
    h = _modulate(xa, mod, 0)
    for l in range(DEPTH):
        i = l // 2
        last = l == DEPTH - 1
        if l % 2 == 0:
            u = _proj(h, ev_w_in, i, 0, EV_Q0, 1024, 0, rope, F32, "proj_hy")
            q = _proj(h, ev_w_in, i, EV_Q0, SWA_H * HD, 1024, SWA_H * HD, rope, BF16, "proj_q")
            kv = _proj(h, ev_w_in, i, EV_K0, 2 * SWA_KV * HD, 512, SWA_KV * HD, rope, BF16, "proj_kv")
            att = _swa(q, kv, swa_sink[i], swa_masks)
            filt = (hy_f_w1[i], hy_f_b1[i], hy_f_w2[i], hy_f_b2[i], hy_f_w3[i], hy_f_freq[i])
            z_lat, z_ctx = _hyena(u, hy_conv_w[i], hy_conv_b[i], filt, hy_bias[i], tables)
            xa, hm, aff = _wout_ln(z_lat, z_ctx, 0, 0, att, 0, ev_w_out[i].astype(BF16), xa, mod, l, lng, lnb,
                                   moe_w_router[l])
        else:
            qkv = _proj(h, od_w_in, i, 0, 3 * D, 1024, 0, rope, BF16, "proj_qkv")
            att = _na(qkv, _na_bias(na_rpb[i]))
            xa, hm, aff = _wout_ln(att, att, L // TM, 0, att, 1, od_w_out[i].astype(BF16), xa, mod, l, lng, lnb,
                                   moe_w_router[l])
        out = _moe(hm, aff, xa, mod, l, lng, lnb, moe_w1, moe_w3, moe_w2, last)
        if last:
            return out[0]
        xa, h = out
```

```python
import functools
import math

import jax
import jax.numpy as jnp
from jax import lax
from jax.experimental import pallas as pl
from jax.experimental.pallas import tpu as pltpu

F32 = jnp.float32
BF16 = jnp.bfloat16
I32 = jnp.int32

D = 2048
NB = 4
L = 2048
LC = 256
S = L + LC
DEPTH = 4
GW = 64
HD = 128
HY = D // 2
HY_BANDS = 16
HY_EMB = 2 * HY_BANDS + 1
HY_HID = 64
HY_FAST_DECAY = 0.3
HY_SLOW_DECAY = 1.5
HY_TARGET = 1e-2
SWA_H = (D - HY) // HD
SWA_KV = SWA_H // 4
SWA_G = SWA_H // SWA_KV
SWA_W = 128
NA_H = D // HD
NA_KH = 8
NA_KW = 16
ROPE_BASE = 10000.0
NE = 16
CAP_L = 2 * L // NE
CAP_C = 2 * LC // NE
CAP = CAP_L + CAP_C
FF = D // 2
DN_ALPHA = (2 * DEPTH) ** 0.25
LN_EPS = 1e-5
EV_Q0 = 3 * HY
EV_K0 = EV_Q0 + SWA_H * HD
EV_IN = EV_K0 + 2 * SWA_KV * HD
NEG = -1e30
LOG2E = math.log2(math.e)

TM = 256
N_LAT_TILES = L // TM
TMW = 3 * TM
VMEM_LIMIT = 56 * 1024 * 1024


def _params(n_axes, vmem=VMEM_LIMIT):
    return pltpu.CompilerParams(dimension_semantics=("arbitrary",) * n_axes, vmem_limit_bytes=vmem)


def _mod_spec(layer, k):
    def index(b, i):
        row = jnp.where(i >= N_LAT_TILES, NB, b)
        return ((layer * 8 + row) * 6 + k, 0, 0)
    return pl.BlockSpec((1, 1, D), index)


def _ada_kernel(c_ref, w_ref, b_ref, o_ref):
    c = c_ref[...]
    s = c * jax.nn.sigmoid(c)
    o_ref[0] = jnp.dot(s.astype(BF16), w_ref[0].astype(BF16), preferred_element_type=F32) + b_ref[0]


def _ada(cc, ada_w, ada_b):
    tn = 1024
    return pl.pallas_call(
        _ada_kernel, grid=(DEPTH, 6 * D // tn),
        in_specs=[pl.BlockSpec((8, D), lambda l, j: (0, 0)),
                  pl.BlockSpec((1, D, tn), lambda l, j: (l, 0, j)),
                  pl.BlockSpec((1, 1, tn), lambda l, j: (l, 0, j))],
        out_specs=pl.BlockSpec((1, 8, tn), lambda l, j: (l, 0, j)),
        out_shape=jax.ShapeDtypeStruct((DEPTH, 8, 6 * D), F32),
        compiler_params=_params(2), name="ada")(cc, ada_w, ada_b.reshape(DEPTH, 1, 6 * D))


def _mod_kernel(x_ref, sc_ref, sh_ref, o_ref):
    o_ref[0] = (x_ref[0] * (1.0 + sc_ref[0]) + sh_ref[0]).astype(o_ref.dtype)


def _modulate(x, mod, layer):
    return pl.pallas_call(
        _mod_kernel, grid=(NB, S // TM),
        in_specs=[pl.BlockSpec((1, TM, D), lambda b, i: (b, i, 0)), _mod_spec(layer, 1), _mod_spec(layer, 0)],
        out_specs=pl.BlockSpec((1, TM, D), lambda b, i: (b, i, 0)),
        out_shape=jax.ShapeDtypeStruct((NB, S, D), BF16),
        compiler_params=_params(2), name="modulate")(x, mod, mod)


def _swap32(x):
    n = x.shape[1]
    lane = lax.broadcasted_iota(I32, x.shape, 1)
    return jnp.where((lane & 32) == 0, pltpu.roll(x, n - 32, 1), pltpu.roll(x, 32, 1))


def _proj_kernel(a_ref, w_ref, cos_ref, sin_ref, o_ref, wbf_ref, *, rope_cols):
    @pl.when((pl.program_id(1) == 0) & (pl.program_id(2) == 0))
    def _():
        wbf_ref[...] = w_ref[...].astype(BF16)

    acc = jnp.dot(a_ref[0], wbf_ref[...], preferred_element_type=F32)
    if rope_cols:
        xr = acc[:, :rope_cols]
        reps = rope_cols // HD
        cos = jnp.concatenate([cos_ref[...]] * reps, axis=1)
        sin = jnp.concatenate([sin_ref[...]] * reps, axis=1)
        xr = xr * cos + _swap32(xr) * sin
        acc = xr if rope_cols == acc.shape[1] else jnp.concatenate([xr, acc[:, rope_cols:]], axis=1)
    o_ref[0] = acc.astype(o_ref.dtype)


def _proj(h, w, layer, col0, ncols, tn, rope_cols, rope, out_dtype, name):
    assert col0 % tn == 0 and ncols % tn == 0
    off = col0 // tn
    cos_t, sin_t = rope
    return pl.pallas_call(
        functools.partial(_proj_kernel, rope_cols=rope_cols),
        grid=(ncols // tn, NB, S // TMW),
        in_specs=[pl.BlockSpec((1, TMW, D), lambda j, b, i: (b, i, 0)),
                  pl.BlockSpec((None, D, tn), lambda j, b, i: (layer, 0, j + off)),
                  pl.BlockSpec((TMW, HD), lambda j, b, i: (i, 0)),
                  pl.BlockSpec((TMW, HD), lambda j, b, i: (i, 0))],
        out_specs=pl.BlockSpec((1, TMW, tn), lambda j, b, i: (b, i, j)),
        out_shape=jax.ShapeDtypeStruct((NB, S, ncols), out_dtype),
        scratch_shapes=[pltpu.VMEM((D, tn), BF16)],
        compiler_params=_params(3), name=name)(h, w, cos_t, sin_t)


def _rope_tables():
    t = jnp.arange(S)
    nf = HD // 4
    inv = ROPE_BASE ** (-2.0 * jnp.arange(nf, dtype=F32) / (HD // 2))
    lat = (t < L)[:, None]
    cos, sin = [], []
    for pos in (t // GW, t % GW):
        ang = pos.astype(F32)[:, None] * inv[None, :]
        c = jnp.where(lat, jnp.cos(ang), 1.0)
        s = jnp.where(lat, jnp.sin(ang), 0.0)
        cos += [c, c]
        sin += [-s, s]
    return jnp.concatenate(cos, axis=1), jnp.concatenate(sin, axis=1)


def _swa_kernel(sink_ref, q_ref, k0_ref, k1_ref, k2_ref, kc_ref, v0_ref, v1_ref, v2_ref, vc_ref, mask_ref, o_ref):
    mask = mask_ref[0]
    rg = lax.broadcasted_iota(I32, (SWA_G * SWA_W, 1), 0) // SWA_W
    for hk in range(SWA_KV):
        ks = slice(hk * HD, (hk + 1) * HD)
        qs = jnp.concatenate([q_ref[0, :, (hk * SWA_G + g) * HD:(hk * SWA_G + g + 1) * HD] for g in range(SWA_G)],
                             axis=0)
        k = jnp.concatenate([k0_ref[0, :, ks], k1_ref[0, :, ks], k2_ref[0, :, ks], kc_ref[0, :, ks]], axis=0)
        v = jnp.concatenate([v0_ref[0, :, ks], v1_ref[0, :, ks], v2_ref[0, :, ks], vc_ref[0, :, ks]], axis=0)
        s = lax.dot_general(qs, k, (((1,), (1,)), ((), ())), preferred_element_type=F32) * (HD ** -0.5 * LOG2E) + mask
        snk = jnp.zeros((SWA_G * SWA_W, 1), F32)
        for g in range(SWA_G):
            snk = jnp.where(rg == g, sink_ref[hk * SWA_G + g] * LOG2E, snk)
        m = jnp.maximum(jnp.max(s, axis=1, keepdims=True), snk)
        p = jnp.exp2(s - m)
        den = jnp.sum(p, axis=1, keepdims=True) + jnp.exp2(snk - m)
        o = jnp.dot(p.astype(BF16), v, preferred_element_type=F32) / den
        for g in range(SWA_G):
            o_ref[0, :, (hk * SWA_G + g) * HD:(hk * SWA_G + g + 1) * HD] = \
                o[g * SWA_W:(g + 1) * SWA_W].astype(o_ref.dtype)


def _swa_masks():
    r = jnp.arange(SWA_W)[:, None]
    c = jnp.arange(3 * SWA_W + LC)[None, :]
    band = (c - r >= 0) & (c - r <= 2 * SWA_W)
    is_ctx = c >= 3 * SWA_W
    variants = [band & (c >= SWA_W), band, band & (c < 2 * SWA_W), jnp.zeros_like(band)]
    m = jnp.stack([jnp.where(is_ctx | v, 0.0, NEG).astype(F32) for v in variants])
    return jnp.tile(m, (1, SWA_G, 1))


def _swa(q, kv, sink, masks):
    nb = L // SWA_W
    kvw = SWA_KV * HD

    def band(j, part):
        return pl.BlockSpec((1, SWA_W, kvw), lambda b, n: (b, jnp.clip(n + j - 1, 0, nb - 1), part))

    def ctx(part):
        return pl.BlockSpec((1, LC, kvw), lambda b, n: (b, L // LC, part))

    def variant(b, n):
        return (jnp.where(n == 0, 0, jnp.where(n < nb - 1, 1, jnp.where(n == nb - 1, 2, 3))), 0, 0)

    qspec = pl.BlockSpec((1, SWA_W, SWA_H * HD), lambda b, n: (b, n, 0))
    return pl.pallas_call(
        _swa_kernel, grid=(NB, S // SWA_W),
        in_specs=[pl.BlockSpec(memory_space=pltpu.SMEM), qspec,
                  band(0, 0), band(1, 0), band(2, 0), ctx(0), band(0, 1), band(1, 1), band(2, 1), ctx(1),
                  pl.BlockSpec((1,) + masks.shape[1:], variant)],
        out_specs=qspec,
        out_shape=jax.ShapeDtypeStruct((NB, S, SWA_H * HD), BF16),
        compiler_params=_params(2), name="swa")(sink, q, kv, kv, kv, kv, kv, kv, kv, kv, masks)


N_DR = 2 * NA_KH - 1
N_DC = 2 * NA_KW - 1
NA_G = 4
NA_WIN = NA_G + NA_KH
NA_QB = NA_G * GW
NA_KB = NA_WIN * GW // NA_QB


def _na_bias_kernel(rpb_ref, o_ref):
    h = pl.program_id(0)
    shape = (GW, 2 * GW)
    lane = lax.broadcasted_iota(I32, shape, 1)
    qc = lax.broadcasted_iota(I32, shape, 0)
    kc = lane & (GW - 1)
    diff = kc - qc + (NA_KW - 1)
    cs = jnp.clip(qc - NA_KW // 2, 0, GW - NA_KW)
    ok = (kc >= cs) & (kc < cs + NA_KW)
    right = lax.broadcasted_iota(I32, (1, 2 * GW), 1) >= GW
    for dd in range(N_DR + 1):
        acc = jnp.zeros(shape, F32)
        for m in range(N_DC):
            lo = rpb_ref[(h * N_DR + dd - 1) * N_DC + m] if dd >= 1 else 0.0
            hi = rpb_ref[(h * N_DR + dd) * N_DC + m] if dd < N_DR else 0.0
            acc = jnp.where(diff == m, jnp.where(right, hi, lo), acc)
        valid = ok & (right if dd == 0 else (~right if dd == N_DR else True))
        o_ref[0, dd] = jnp.where(valid, acc * LOG2E, NEG)


def _na_bias(rpb):
    return pl.pallas_call(
        _na_bias_kernel, grid=(NA_H,),
        in_specs=[pl.BlockSpec(memory_space=pltpu.SMEM)],
        out_specs=pl.BlockSpec((1, N_DR + 1, GW, 2 * GW), lambda h: (h, 0, 0, 0)),
        out_shape=jax.ShapeDtypeStruct((NA_H, N_DR + 1, GW, 2 * GW), F32),
        compiler_params=_params(1), name="na_bias")(rpb.reshape(-1))


def _na_kernel(*refs):
    q_ref = refs[0]
    k_refs = refs[1:1 + NA_KB]
    v_refs = refs[1 + NA_KB:1 + 2 * NA_KB]
    kc_ref, vc_ref, tz_ref, o_ref = refs[1 + 2 * NA_KB:]
    scale = HD ** -0.5 * LOG2E
    nt = (((1,), (1,)), ((), ()))
    rows = L // GW
    g = pl.program_id(1)
    is_lat = g < rows // NA_G
    r0 = g * NA_G
    ustart = jnp.clip(r0 - NA_KH // 2, 0, rows - NA_WIN)
    lane = lax.broadcasted_iota(I32, (1, NA_WIN * GW), 1)
    masks, dds = [], []
    for a in range(NA_G):
        r = r0 + a
        lo = (jnp.clip(r - NA_KH // 2, 0, rows - NA_KH) - ustart) * GW
        hi = jnp.where(is_lat, lo + NA_KH * GW, lo)
        masks.append((lane >= lo) & (lane < hi))
        dds.append([jnp.clip(ustart + 2 * p - r + NA_KH, 0, N_DR) for p in range(NA_WIN // 2)])
    for h in range(NA_H):
        hs = slice(h * HD, (h + 1) * HD)
        qh = q_ref[0, :, hs]
        kh = jnp.concatenate([r[0, :, hs] for r in k_refs], axis=0)
        vh = jnp.concatenate([r[0, :, hs] for r in v_refs], axis=0)
        s = lax.dot_general(qh, kh, nt, preferred_element_type=F32) * scale
        parts = []
        for a in range(NA_G):
            bias = jnp.concatenate([tz_ref[h, dd] for dd in dds[a]], axis=1)
            parts.append(jnp.where(masks[a], s[a * GW:(a + 1) * GW] + bias, NEG))
        s = jnp.concatenate(parts, axis=0)
        sc = lax.dot_general(qh, kc_ref[0, :, hs], nt, preferred_element_type=F32) * scale
        m = jnp.maximum(jnp.max(s, axis=1, keepdims=True), jnp.max(sc, axis=1, keepdims=True))
        p = jnp.exp2(s - m)
        pc = jnp.exp2(sc - m)
        den = jnp.sum(p, axis=1, keepdims=True) + jnp.sum(pc, axis=1, keepdims=True)
        o = jnp.dot(p.astype(BF16), vh, preferred_element_type=F32)
        o = o + jnp.dot(pc.astype(BF16), vc_ref[0, :, hs], preferred_element_type=F32)
        o_ref[0, :, hs] = (o / den).astype(o_ref.dtype)


def _na(qkv, tz):
    rows = L // GW
    assert LC == NA_QB and (rows - NA_WIN) % NA_G == 0

    def win(j, part):
        return pl.BlockSpec(
            (1, NA_QB, D), lambda b, g: (b, jnp.clip(g * NA_G - NA_KH // 2, 0, rows - NA_WIN) // NA_G + j, part))

    in_specs = [pl.BlockSpec((1, NA_QB, D), lambda b, g: (b, g, 0))]
    in_specs += [win(j, 1) for j in range(NA_KB)] + [win(j, 2) for j in range(NA_KB)]
    in_specs += [pl.BlockSpec((1, LC, D), lambda b, g: (b, L // LC, 1)),
                 pl.BlockSpec((1, LC, D), lambda b, g: (b, L // LC, 2)),
                 pl.BlockSpec(tz.shape, lambda b, g: (0, 0, 0, 0))]
    return pl.pallas_call(
        _na_kernel, grid=(NB, S // NA_QB),
        in_specs=in_specs,
        out_specs=pl.BlockSpec((1, NA_QB, D), lambda b, g: (b, g, 0)),
        out_shape=jax.ShapeDtypeStruct((NB, S, D), BF16),
        compiler_params=_params(2), name="na")(*([qkv] * (3 + 2 * NA_KB)), tz)


def _filter_kernel(z_ref, w1_ref, b1_ref, w2_ref, b2_ref, w3_ref, fr_ref, dl_ref, o_ref):
    hp = lax.Precision.HIGHEST
    z = z_ref[...]
    fr = fr_ref[...]
    h = jnp.sin(fr * (jnp.dot(z, w1_ref[...], precision=hp, preferred_element_type=F32) + b1_ref[...]))
    for i in range(w2_ref.shape[0]):
        h = jnp.sin(fr * (jnp.dot(h, w2_ref[i], precision=hp, preferred_element_type=F32) + b2_ref[i]))
    h = jnp.dot(h, w3_ref[...], precision=hp, preferred_element_type=F32)
    decay = jnp.exp(-z[:, 0:1] * dl_ref[...])
    parts = []
    for order in range(2):
        fwd = h[:, (2 * order) * HY:(2 * order + 1) * HY] * decay
        bwd = h[:, (2 * order + 1) * HY:(2 * order + 2) * HY] * decay
        parts += [fwd + bwd, fwd - bwd]
    o_ref[...] = jnp.concatenate(parts, axis=1).astype(o_ref.dtype)


def _filter_embedding(n):
    t = jnp.linspace(0.0, 1.0, n, dtype=F32)[:, None]
    w = (2.0 * math.pi / n) * jnp.arange(n, dtype=F32)[:, None]
    f = jnp.linspace(1e-4, HY_BANDS - 1, HY_BANDS, dtype=F32)[None, :]
    z = jnp.concatenate([t, jnp.cos(f * w), -jnp.sin(f * w)], axis=-1)
    return jnp.pad(z, ((0, 0), (0, HD - HY_EMB)))


def _hyena_filters(n, fw1, fb1, fw2, fb2, fw3, ffreq):
    deltas = jnp.abs(jnp.linspace(math.log(HY_TARGET) / HY_SLOW_DECAY, math.log(HY_TARGET) / HY_FAST_DECAY,
                                  HY, dtype=F32))[None, :]
    w1 = jnp.pad(fw1, ((0, HD - HY_EMB), (0, 0)))
    tm = min(n, 256)
    nout = fw3.shape[1]
    full = lambda a: pl.BlockSpec(a.shape, lambda i: (0,) * a.ndim)
    args = (_filter_embedding(n), w1, fb1.reshape(1, -1), fw2, fb2.reshape(fb2.shape[0], 1, -1), fw3,
            ffreq.reshape(1, -1), deltas)
    return pl.pallas_call(
        _filter_kernel, grid=(n // tm,),
        in_specs=[pl.BlockSpec((tm, HD), lambda i: (i, 0))] + [full(a) for a in args[1:]],
        out_specs=pl.BlockSpec((tm, nout), lambda i: (i, 0)),
        out_shape=jax.ShapeDtypeStruct((n, nout), BF16),
        compiler_params=_params(1), name="hy_filter")(*args)


def _shortconv_kernel(u_ref, w_ref, b_ref, o_ref):
    u = u_ref[0]
    t = lax.broadcasted_iota(I32, (S, 1), 0)
    prev = jnp.where((t == 0) | (t == L), 0.0, pltpu.roll(u, 1, 0))
    nxt = jnp.where((t == L - 1) | (t == S - 1), 0.0, pltpu.roll(u, S - 1, 0))
    w = w_ref[...]
    o_ref[0] = b_ref[...] + prev * w[0:1] + u * w[1:2] + nxt * w[2:3]


def _shortconv(u, w, b):
    tc = 256
    nc = u.shape[2]
    return pl.pallas_call(
        _shortconv_kernel, grid=(NB, nc // tc),
        in_specs=[pl.BlockSpec((1, S, tc), lambda b_, c: (b_, 0, c)),
                  pl.BlockSpec((3, tc), lambda b_, c: (0, c)),
                  pl.BlockSpec((1, tc), lambda b_, c: (0, c))],
        out_specs=pl.BlockSpec((1, S, tc), lambda b_, c: (b_, 0, c)),
        out_shape=jax.ShapeDtypeStruct(u.shape, F32),
        compiler_params=_params(2), name="shortconv")(u, w, b.reshape(1, -1))


def _dft_tables(n):
    nn = 2 * n
    lo = 64
    f = jnp.arange(n, dtype=I32)[:, None]
    angle = lambda m: (m % nn).astype(F32) * (2.0 * math.pi / nn)
    a = angle(f * (jnp.arange(n // lo, dtype=I32) * lo)[None, :])[:, :, None]
    b = angle(f * jnp.arange(lo, dtype=I32)[None, :])[:, None, :]
    c = (jnp.cos(a) * jnp.cos(b) - jnp.sin(a) * jnp.sin(b)).reshape(n, n)
    s = -(jnp.sin(a) * jnp.cos(b) + jnp.cos(a) * jnp.sin(b)).reshape(n, n)
    i = jnp.arange(n, dtype=I32)
    alt = jnp.where(i % 2 == 0, 1.0, -1.0).astype(F32)
    fwd = jnp.concatenate([c, jnp.where(i[:, None] == 0, alt[None, :], s)], axis=0)
    inv_r = c * jnp.where(i == 0, 1.0 / nn, 2.0 / nn).astype(F32)[None, :]
    inv_i = jnp.where(i[None, :] == 0, alt[:, None] / nn, s * (2.0 / nn))
    return fwd.astype(BF16), inv_r.astype(BF16), inv_i.astype(BF16)


HY_TN = 512
HY_NCT = HY // HY_TN


def _filt_dft_kernel(c_ref, s_ref, hs_ref, hd_ref, kr_ref, ki_ref):
    kr_ref[0] = jnp.dot(c_ref[...], hs_ref[...], preferred_element_type=F32)
    ki_ref[0] = jnp.dot(s_ref[...], hd_ref[...], preferred_element_type=F32)

    @pl.when(pl.program_id(0) == 0)
    def _():
        nyq = jnp.dot(s_ref[0:8, :], hs_ref[...], preferred_element_type=F32)
        first = lax.broadcasted_iota(I32, nyq.shape, 0) == 0
        ki_ref[0, 0:8, :] = jnp.where(first, nyq, ki_ref[0, 0:8, :])


def _filt_dft(fwd, hfilt, n):
    tf = min(n, 1024)
    nf = n // tf
    out = pl.BlockSpec((1, tf, HY_TN), lambda f, c: (c // HY_NCT, f, c % HY_NCT))
    return pl.pallas_call(
        _filt_dft_kernel, grid=(nf, 2 * HY_NCT),
        in_specs=[pl.BlockSpec((tf, n), lambda f, c: (f, 0)),
                  pl.BlockSpec((tf, n), lambda f, c: (nf + f, 0)),
                  pl.BlockSpec((n, HY_TN), lambda f, c: (0, (c // HY_NCT) * 2 * HY_NCT + c % HY_NCT)),
                  pl.BlockSpec((n, HY_TN), lambda f, c: (0, (c // HY_NCT) * 2 * HY_NCT + HY_NCT + c % HY_NCT))],
        out_specs=[out, out],
        out_shape=[jax.ShapeDtypeStruct((2, n, HY), F32)] * 2,
        compiler_params=_params(2), name=f"filt_dft{n}")(fwd, fwd, hfilt, hfilt)


def _dft_mul_kernel(z_ref, c_ref, s_ref, kr_ref, ki_ref, yr_ref, yi_ref):
    z = z_ref[0].astype(BF16)
    zr = jnp.dot(c_ref[...], z, preferred_element_type=F32)
    zi = jnp.dot(s_ref[...], z, preferred_element_type=F32)
    kr = kr_ref[...]
    ki = ki_ref[...]
    first = (lax.broadcasted_iota(I32, zr.shape, 0) == 0) & (pl.program_id(0) == 0)
    ii = zi * ki
    yr_ref[0] = (zr * kr - jnp.where(first, 0.0, ii)).astype(yr_ref.dtype)
    yi_ref[0] = jnp.where(first, ii, zr * ki + zi * kr).astype(yi_ref.dtype)


def _dft_mul(z, zrow, zc0, n, fwd, kr, ki, order):
    tf = min(n, 1024)
    nf = n // tf
    out = pl.BlockSpec((1, tf, HY_TN), lambda f, b, c: (b, f, c))
    kspec = lambda: pl.BlockSpec((None, tf, HY_TN), lambda f, b, c: (order, f, c))
    return pl.pallas_call(
        _dft_mul_kernel, grid=(nf, NB, HY_NCT),
        in_specs=[pl.BlockSpec((1, n, HY_TN), lambda f, b, c: (b, zrow, zc0 + c)),
                  pl.BlockSpec((tf, n), lambda f, b, c: (f, 0)),
                  pl.BlockSpec((tf, n), lambda f, b, c: (nf + f, 0)),
                  kspec(), kspec()],
        out_specs=[out, out],
        out_shape=[jax.ShapeDtypeStruct((NB, n, HY), BF16)] * 2,
        compiler_params=_params(3), name=f"dft_mul{n}")(z, fwd, fwd, kr, ki)


def _idft_gate_kernel(ar_ref, ai_ref, yr_ref, yi_ref, z_ref, p_ref, bias_ref, *o_refs):
    y = jnp.dot(ar_ref[...], yr_ref[0], preferred_element_type=F32)
    y = y + jnp.dot(ai_ref[...], yi_ref[0], preferred_element_type=F32)
    out = p_ref[0] * (y + z_ref[0] * bias_ref[0])
    for o_ref in o_refs:
        o_ref[0] = out.astype(o_ref.dtype)


def _idft_gate(yr, yi, inv_r, inv_i, n, z, z_r0, z_c0, p, p_r0, p_c0, bias, order, out_dtype):
    tm = min(n, 1024)
    return pl.pallas_call(
        _idft_gate_kernel, grid=(n // tm, NB, HY_NCT),
        in_specs=[pl.BlockSpec((tm, n), lambda t, b, c: (t, 0)),
                  pl.BlockSpec((tm, n), lambda t, b, c: (t, 0)),
                  pl.BlockSpec((1, n, HY_TN), lambda t, b, c: (b, 0, c)),
                  pl.BlockSpec((1, n, HY_TN), lambda t, b, c: (b, 0, c)),
                  pl.BlockSpec((1, tm, HY_TN), lambda t, b, c: (b, z_r0 + t, z_c0 + c)),
                  pl.BlockSpec((1, tm, HY_TN), lambda t, b, c: (b, p_r0 + t, p_c0 + c)),
                  pl.BlockSpec((1, 1, HY_TN), lambda t, b, c: (order, 0, c))],
        out_specs=pl.BlockSpec((1, tm, HY_TN), lambda t, b, c: (b, t, c)),
        out_shape=jax.ShapeDtypeStruct((NB, n, HY), out_dtype),
        compiler_params=_params(3), name=f"idft_gate{n}_{order}")(inv_r, inv_i, yr, yi, z, p, bias)


def _hyena(u, conv_w, conv_b, filt, hy_bias, tables):
    sc = _shortconv(u, conv_w, conv_b)
    bias = hy_bias.reshape(2, 1, HY)
    outs = []
    for n in (L, LC):
        zrow = 0 if n == L else L // LC
        r0 = zrow * (n // min(n, 1024))
        fwd, inv_r, inv_i = tables[n]
        kr, ki = _filt_dft(fwd, _hyena_filters(n, *filt), n)
        yr, yi = _dft_mul(sc, zrow, 2 * HY_NCT, n, fwd, kr, ki, 0)
        z1 = _idft_gate(yr, yi, inv_r, inv_i, n, sc, r0, 2 * HY_NCT, sc, r0, 0, bias, 0, F32)
        yr, yi = _dft_mul(z1, 0, 0, n, fwd, kr, ki, 1)
        outs.append(_idft_gate(yr, yi, inv_r, inv_i, n, z1, 0, 0, sc, r0, HY_NCT, bias, 1, BF16))
    return outs


def _resid_ln(x, y, gate, g, b):
    v = DN_ALPHA * x + gate * y
    mu = jnp.mean(v, axis=-1, keepdims=True)
    vc = v - mu
    var = jnp.mean(vc * vc, axis=-1, keepdims=True)
    return vc * lax.rsqrt(var + LN_EPS) * g + b


def _wout_ln_kernel(a1a_ref, a1b_ref, a1c_ref, a1x_ref, a2_ref, w_ref, x_ref, gl_ref, gc_ref, g_ref, b_ref,
                    scl_ref, scc_ref, shl_ref, shc_ref, wrh_ref, wrl_ref, xo_ref, hm_ref, aff_ref):
    half = a2_ref.shape[2]
    last_tile = pl.program_id(1) == pl.num_programs(1) - 1
    a1_refs = (a1a_ref, a1b_ref, a1c_ref)
    n_sub = len(a1_refs)

    def project(s):
        a1 = jnp.where(last_tile, a1x_ref[0], a1_refs[s][0]) if s == n_sub - 1 else a1_refs[s][0]
        y = jnp.dot(a1, w_ref[0:half, :], preferred_element_type=F32)
        return y + jnp.dot(a2_ref[0, s * TM:(s + 1) * TM, :], w_ref[half:2 * half, :], preferred_element_type=F32)

    for s in range(n_sub):
        rows = slice(s * TM, (s + 1) * TM)
        if s == n_sub - 1:
            pick = lambda lat_ref, ctx_ref: jnp.where(last_tile, ctx_ref[0], lat_ref[0])
        else:
            pick = lambda lat_ref, ctx_ref: lat_ref[0]
        xn = _resid_ln(x_ref[0, rows, :], project(s), pick(gl_ref, gc_ref), g_ref[0], b_ref[0])
        xo_ref[0, rows, :] = xn
        hm = xn * (1.0 + pick(scl_ref, scc_ref)) + pick(shl_ref, shc_ref)
        hm_hi = hm.astype(BF16)
        hm_ref[0, rows, :] = hm_hi
        hm_lo = (hm - hm_hi.astype(F32)).astype(BF16)
        logits = jnp.dot(hm_hi, wrh_ref[...], preferred_element_type=F32)
        logits = logits + jnp.dot(hm_lo, wrh_ref[...], preferred_element_type=F32)
        logits = logits + jnp.dot(hm_hi, wrl_ref[...], preferred_element_type=F32)
        e = jnp.exp(logits - jnp.max(logits, axis=-1, keepdims=True))
        aff_ref[0, rows, :] = e / jnp.sum(e, axis=-1, keepdims=True)


def _wout_ln(a1_lat, a1_ctx, a1_ctx_blk, c1, a2, c2, w_bf, x, mod, layer, lng, lnb, w_router):
    half = D // 2
    sub = TMW // TM
    n_lat_blk = a1_lat.shape[1] // TM
    tile = lambda: pl.BlockSpec((1, TMW, D), lambda b, i: (b, i, 0))
    vec = lambda idx: pl.BlockSpec((1, 1, D), idx)
    lat = lambda k: vec(lambda b, i: ((layer * 8 + b) * 6 + k, 0, 0))
    ctx = lambda k: vec(lambda b, i: ((layer * 8 + NB) * 6 + k, 0, 0))
    a1_blk = lambda s: pl.BlockSpec((1, TM, half), lambda b, i: (b, jnp.minimum(i * sub + s, n_lat_blk - 1), c1))
    wr_hi = w_router.astype(BF16)
    wr_lo = (w_router - wr_hi.astype(F32)).astype(BF16)
    return pl.pallas_call(
        _wout_ln_kernel, grid=(NB, S // TMW),
        in_specs=[a1_blk(0), a1_blk(1), a1_blk(2),
                  pl.BlockSpec((1, TM, half), lambda b, i: (b, a1_ctx_blk, c1)),
                  pl.BlockSpec((1, TMW, half), lambda b, i: (b, i, c2)),
                  pl.BlockSpec((D, D), lambda b, i: (0, 0), pipeline_mode=pl.Buffered(1)),
                  tile(), lat(2), ctx(2),
                  vec(lambda b, i: (2 * layer, 0, 0)), vec(lambda b, i: (2 * layer, 0, 0)),
                  lat(4), ctx(4), lat(3), ctx(3),
                  pl.BlockSpec((D, NE), lambda b, i: (0, 0)),
                  pl.BlockSpec((D, NE), lambda b, i: (0, 0))],
        out_specs=[tile(), tile(), pl.BlockSpec((1, TMW, NE), lambda b, i: (b, i, 0))],
        out_shape=[jax.ShapeDtypeStruct((NB, S, D), F32), jax.ShapeDtypeStruct((NB, S, D), BF16),
                   jax.ShapeDtypeStruct((NB, S, NE), F32)],
        compiler_params=_params(2), name="wout_ln")(
            a1_lat, a1_lat, a1_lat, a1_ctx, a2, w_bf, x, mod, mod, lng, lnb, mod, mod, mod, mod, wr_hi, wr_lo)


def _route_kernel(a_ref, o_ref, tri_ref):
    rb = 256
    for r in range(L // rb):
        s_i = lax.broadcasted_iota(I32, (rb, L), 0) + r * rb
        t_i = lax.broadcasted_iota(I32, (rb, L), 1)
        tri_ref[r * rb:(r + 1) * rb, :] = jnp.where(s_i < t_i, 1.0, 0.0).astype(BF16)

    def select(a, cap, base):
        n = a.shape[1]

        def body(i, prefix):
            cand = prefix | jnp.left_shift(jnp.int32(1), 30 - i)
            cnt = jnp.sum(jnp.where(a >= lax.bitcast_convert_type(cand, F32), 1.0, 0.0), axis=1, keepdims=True)
            return jnp.where(cnt >= cap, cand, prefix)

        kth = lax.bitcast_convert_type(lax.fori_loop(0, 31, body, jnp.zeros((a.shape[0], 1), I32)), F32)
        gt = a > kth
        eq = a == kth
        n_gt = jnp.sum(jnp.where(gt, 1.0, 0.0), axis=1, keepdims=True)
        tri = tri_ref[0:n, 0:n]
        before = jnp.dot(jnp.where(eq, 1.0, 0.0).astype(BF16), tri, preferred_element_type=F32)
        sel = gt | (eq & (before < cap - n_gt))
        pos = jnp.dot(jnp.where(sel, 1.0, 0.0).astype(BF16), tri, preferred_element_type=F32)
        return jnp.where(sel, pos.astype(I32) + base, -1)

    a = a_ref[...]
    o_ref[...] = jnp.concatenate([select(a[:, :L], CAP_L, 0), select(a[:, L:], CAP_C, CAP_L)], axis=1)


def _route(aff_t):
    return pl.pallas_call(
        _route_kernel, grid=(1,),
        in_specs=[pl.BlockSpec((NB * NE, S), lambda i: (0, 0))],
        out_specs=pl.BlockSpec((NB * NE, S), lambda i: (0, 0)),
        out_shape=jax.ShapeDtypeStruct((NB * NE, S), I32),
        scratch_shapes=[pltpu.VMEM((L, L), BF16)],
        compiler_params=_params(1), name="route")(aff_t.reshape(NB * NE, S)).reshape(NB, NE, S)


def _gather_kernel(slot_ref, aff_ref, h_ref, xs_ref, g_ref):
    slot = slot_ref[0, 0]
    aff = aff_ref[0, 0]
    for r0, cap, c0, n in ((0, CAP_L, 0, L), (CAP_L, CAP_C, L, LC)):
        hit = slot[:, c0:c0 + n] == lax.broadcasted_iota(I32, (cap, n), 0) + r0
        xs = jnp.dot(jnp.where(hit, 1.0, 0.0).astype(BF16), h_ref[0, c0:c0 + n, :], preferred_element_type=F32)
        xs_ref[0, 0, r0:r0 + cap, :] = xs.astype(xs_ref.dtype)
        g = jnp.sum(jnp.where(hit, aff[:, c0:c0 + n], 0.0), axis=1, keepdims=True)
        g_ref[0, 0, r0:r0 + cap, :] = jnp.broadcast_to(g, (cap, HD))


def _gather(slot, aff_t, hm):
    row = lambda: pl.BlockSpec((1, 1, 1, S), lambda b, e: (b, e, 0, 0))
    return pl.pallas_call(
        _gather_kernel, grid=(NB, NE),
        in_specs=[row(), row(), pl.BlockSpec((1, S, D), lambda b, e: (b, 0, 0))],
        out_specs=[pl.BlockSpec((1, 1, CAP, D), lambda b, e: (b, e, 0, 0)),
                   pl.BlockSpec((1, 1, CAP, HD), lambda b, e: (b, e, 0, 0))],
        out_shape=[jax.ShapeDtypeStruct((NB, NE, CAP, D), BF16), jax.ShapeDtypeStruct((NB, NE, CAP, HD), F32)],
        compiler_params=_params(2), name="moe_gather")(
            slot.reshape(NB, NE, 1, S), aff_t.reshape(NB, NE, 1, S), hm)


FFN_SPLIT = 2


def _ffn_kernel(xs_ref, g_ref, w1_ref, w3_ref, w2_ref, ol_ref, oc_ref, hmid_ref):
    f = pl.program_id(1)
    tf = FF // FFN_SPLIT

    @pl.when(f < FFN_SPLIT)
    def _():
        x = xs_ref[:, 0].reshape(NB * CAP, D)
        a = jnp.dot(x, w1_ref[0].astype(BF16), preferred_element_type=F32)
        u = jnp.dot(x, w3_ref[0].astype(BF16), preferred_element_type=F32)
        hmid_ref[:, pl.ds(pl.multiple_of(f * tf, tf), tf)] = (a * jax.nn.sigmoid(a) * u).astype(BF16)

    @pl.when(f >= FFN_SPLIT)
    def _():
        y = jnp.dot(hmid_ref[...], w2_ref[0].astype(BF16), preferred_element_type=F32)
        g = g_ref[:, 0].reshape(NB * CAP, HD)[:, 0:1]
        y = (y * g).astype(ol_ref.dtype).reshape(NB, CAP, y.shape[1])
        ol_ref[:, 0] = y[:, :CAP_L]
        oc_ref[:, 0] = y[:, CAP_L:]


def _ffn(xs, gate, w1, w3, w2, layer):
    tf = FF // FFN_SPLIT
    tn = D // FFN_SPLIT
    blk = lambda width: pl.BlockSpec((NB, 1, CAP, width), lambda e, f: (0, e, 0, 0))
    hid = lambda e, f: (layer, e, 0, jnp.minimum(f, FFN_SPLIT - 1))
    col = lambda f: jnp.maximum(f - FFN_SPLIT, 0)
    out = lambda cap: pl.BlockSpec((NB, 1, cap, tn), lambda e, f: (0, e, 0, col(f)))
    return pl.pallas_call(
        _ffn_kernel, grid=(NE, 2 * FFN_SPLIT),
        in_specs=[blk(D), blk(HD),
                  pl.BlockSpec((None, 1, D, tf), hid),
                  pl.BlockSpec((None, 1, D, tf), hid),
                  pl.BlockSpec((None, 1, FF, tn), lambda e, f: (layer, e, 0, col(f)))],
        out_specs=[out(CAP_L), out(CAP_C)],
        out_shape=[jax.ShapeDtypeStruct((NB, NE, CAP_L, D), BF16), jax.ShapeDtypeStruct((NB, NE, CAP_C, D), BF16)],
        scratch_shapes=[pltpu.VMEM((NB * CAP, FF), BF16)],
        compiler_params=_params(2), name="moe_ffn")(xs, gate, w1, w3, w2)


def _combine_ln_kernel(*refs, last):
    if last:
        slot_ref, yl_ref, x_ref, gate_ref, g_ref, b_ref, xo_ref, acc_ref = refs
    else:
        slot_ref, yl_ref, yc_ref, x_ref, gate_ref, g_ref, b_ref, sc_ref, sh_ref, xo_ref, hn_ref, acc_ref = refs
    st = slot_ref[0]

    def scatter(y_ref, r0, cap):
        want = lax.broadcasted_iota(I32, (TM, cap), 1) + r0
        hits = [jnp.where(st[:, e:e + 1] == want, 1.0, 0.0).astype(BF16) for e in range(NE)]
        if cap % HD == 0:
            acc_ref[...] = jnp.dot(jnp.concatenate(hits, axis=1), y_ref[0].reshape(NE * cap, D),
                                   preferred_element_type=F32)
        else:
            acc = None
            for e in range(NE):
                part = jnp.dot(hits[e], y_ref[0, e], preferred_element_type=F32)
                acc = part if acc is None else acc + part
            acc_ref[...] = acc

    if last:
        scatter(yl_ref, 0, CAP_L)
    else:
        is_lat = pl.program_id(1) < N_LAT_TILES
        pl.when(is_lat)(lambda: scatter(yl_ref, 0, CAP_L))
        pl.when(jnp.logical_not(is_lat))(lambda: scatter(yc_ref, CAP_L, CAP_C))
    xn = _resid_ln(x_ref[0], acc_ref[...], gate_ref[0], g_ref[0], b_ref[0])
    xo_ref[0] = xn
    if not last:
        hn_ref[0] = (xn * (1.0 + sc_ref[0]) + sh_ref[0]).astype(hn_ref.dtype)


def _combine_ln(slot_t, yl, yc, x, mod, layer, lng, lnb, last):
    rows = L if last else S
    tile = lambda: pl.BlockSpec((1, TM, D), lambda b, i: (b, i, 0))
    ln_vec = lambda: pl.BlockSpec((1, 1, D), lambda b, i: (2 * layer + 1, 0, 0))
    in_specs = [pl.BlockSpec((1, TM, NE), lambda b, i: (b, i, 0)),
                pl.BlockSpec((1, NE, CAP_L, D), lambda b, i: (b, 0, 0, 0), pipeline_mode=pl.Buffered(1))]
    args = [slot_t, yl]
    if not last:
        in_specs.append(pl.BlockSpec((1, NE, CAP_C, D), lambda b, i: (b, 0, 0, 0)))
        args.append(yc)
    in_specs += [tile(), _mod_spec(layer, 5), ln_vec(), ln_vec()]
    args += [x, mod, lng, lnb]
    out_specs = [tile()]
    out_shape = [jax.ShapeDtypeStruct((NB, rows, D), F32)]
    if not last:
        in_specs += [_mod_spec(layer + 1, 1), _mod_spec(layer + 1, 0)]
        args += [mod, mod]
        out_specs.append(tile())
        out_shape.append(jax.ShapeDtypeStruct((NB, rows, D), BF16))
    return pl.pallas_call(
        functools.partial(_combine_ln_kernel, last=last), grid=(NB, rows // TM),
        in_specs=in_specs, out_specs=out_specs, out_shape=out_shape,
        scratch_shapes=[pltpu.VMEM((TM, D), F32)],
        compiler_params=_params(2), name="moe_combine_ln")(*args)


def _moe(hm, aff, x, mod, layer, lng, lnb, w1, w3, w2, last):
    aff_t = jnp.swapaxes(aff, 1, 2)
    slot = _route(aff_t)
    xs, gate = _gather(slot, aff_t, hm)
    yl, yc = _ffn(xs, gate, w1, w3, w2, layer)
    return _combine_ln(jnp.swapaxes(slot, 1, 2), yl, yc, x, mod, layer, lng, lnb, last)


def kernel(x, c, ctx, c_ctx, ada_w, ada_b, ln_g, ln_b, ev_w_in, ev_w_out, hy_conv_w, hy_conv_b, hy_f_w1, hy_f_b1,
           hy_f_w2, hy_f_b2, hy_f_w3, hy_f_freq, hy_bias, swa_sink, od_w_in, od_w_out, na_rpb, moe_w_router,
           moe_w1, moe_w3, moe_w2):
    assert x.shape == (NB, L, D) and ctx.shape == (NB, LC, D)
    xa = jnp.concatenate([x, ctx], axis=1)
    cc = jnp.concatenate([c, c_ctx[None, :], jnp.zeros((8 - NB - 1, D), F32)], axis=0)
    mod = _ada(cc, ada_w, ada_b).reshape(DEPTH * 8 * 6, 1, D)
    lng = ln_g.reshape(DEPTH * 2, 1, D)
    lnb = ln_b.reshape(DEPTH * 2, 1, D)
    rope = _rope_tables()
    tables = {L: _dft_tables(L), LC: _dft_tables(LC)}
    swa_masks = _swa_masks()
    h = _modulate(xa, mod, 0)
    for l in range(DEPTH):
        i = l // 2
        last = l == DEPTH - 1
        if l % 2 == 0:
            u = _proj(h, ev_w_in, i, 0, EV_Q0, 1024, 0, rope, F32, "proj_hy")
            q = _proj(h, ev_w_in, i, EV_Q0, SWA_H * HD, 1024, SWA_H * HD, rope, BF16, "proj_q")
            kv = _proj(h, ev_w_in, i, EV_K0, 2 * SWA_KV * HD, 512, SWA_KV * HD, rope, BF16, "proj_kv")
            att = _swa(q, kv, swa_sink[i], swa_masks)
            filt = (hy_f_w1[i], hy_f_b1[i], hy_f_w2[i], hy_f_b2[i], hy_f_w3[i], hy_f_freq[i])
            z_lat, z_ctx = _hyena(u, hy_conv_w[i], hy_conv_b[i], filt, hy_bias[i], tables)
            xa, hm, aff = _wout_ln(z_lat, z_ctx, 0, 0, att, 0, ev_w_out[i].astype(BF16), xa, mod, l, lng, lnb,
                                   moe_w_router[l])
        else:
            qkv = _proj(h, od_w_in, i, 0, 3 * D, 1024, 0, rope, BF16, "proj_qkv")
            att = _na(qkv, _na_bias(na_rpb[i]))
            xa, hm, aff = _wout_ln(att, att, L // TM, 0, att, 1, od_w_out[i].astype(BF16), xa, mod, l, lng, lnb,
                                   moe_w_router[l])
        out = _moe(hm, aff, xa, mod, l, lng, lnb, moe_w1, moe_w3, moe_w2, last)
        if last:
            return out[0]
        xa, h = out
```

```python
import functools
import math

import jax
import jax.numpy as jnp
from jax import lax
from jax.experimental import pallas as pl
from jax.experimental.pallas import tpu as pltpu

F32 = jnp.float32
BF16 = jnp.bfloat16
I32 = jnp.int32

D = 2048
NB = 4
L = 2048
LC = 256
S = L + LC
DEPTH = 4
GW = 64
HD = 128
HY = D // 2
HY_BANDS = 16
HY_EMB = 2 * HY_BANDS + 1
HY_HID = 64
HY_FAST_DECAY = 0.3
HY_SLOW_DECAY = 1.5
HY_TARGET = 1e-2
SWA_H = (D - HY) // HD
SWA_KV = SWA_H // 4
SWA_G = SWA_H // SWA_KV
SWA_W = 128
NA_H = D // HD
NA_KH = 8
NA_KW = 16
ROPE_BASE = 10000.0
NE = 16
CAP_L = 2 * L // NE
CAP_C = 2 * LC // NE
CAP = CAP_L + CAP_C
FF = D // 2
DN_ALPHA = (2 * DEPTH) ** 0.25
LN_EPS = 1e-5
EV_Q0 = 3 * HY
EV_K0 = EV_Q0 + SWA_H * HD
EV_IN = EV_K0 + 2 * SWA_KV * HD
NEG = -1e30
LOG2E = math.log2(math.e)

TM = 256
N_LAT_TILES = L // TM
TMW = 3 * TM
VMEM_LIMIT = 56 * 1024 * 1024


def _params(n_axes, vmem=VMEM_LIMIT):
    return pltpu.CompilerParams(dimension_semantics=("arbitrary",) * n_axes, vmem_limit_bytes=vmem)


def _mod_spec(layer, k):
    def index(b, i):
        row = jnp.where(i >= N_LAT_TILES, NB, b)
        return ((layer * 8 + row) * 6 + k, 0, 0)
    return pl.BlockSpec((1, 1, D), index)


def _ada_kernel(c_ref, w_ref, b_ref, o_ref):
    c = c_ref[...]
    s = c * jax.nn.sigmoid(c)
    o_ref[0] = jnp.dot(s.astype(BF16), w_ref[0].astype(BF16), preferred_element_type=F32) + b_ref[0]


def _ada(cc, ada_w, ada_b):
    tn = 1024
    return pl.pallas_call(
        _ada_kernel, grid=(DEPTH, 6 * D // tn),
        in_specs=[pl.BlockSpec((8, D), lambda l, j: (0, 0)),
                  pl.BlockSpec((1, D, tn), lambda l, j: (l, 0, j)),
                  pl.BlockSpec((1, 1, tn), lambda l, j: (l, 0, j))],
        out_specs=pl.BlockSpec((1, 8, tn), lambda l, j: (l, 0, j)),
        out_shape=jax.ShapeDtypeStruct((DEPTH, 8, 6 * D), F32),
        compiler_params=_params(2), name="ada")(cc, ada_w, ada_b.reshape(DEPTH, 1, 6 * D))


def _mod_kernel(x_ref, sc_ref, sh_ref, o_ref):
    o_ref[0] = (x_ref[0] * (1.0 + sc_ref[0]) + sh_ref[0]).astype(o_ref.dtype)


def _modulate(x, mod, layer):
    return pl.pallas_call(
        _mod_kernel, grid=(NB, S // TM),
        in_specs=[pl.BlockSpec((1, TM, D), lambda b, i: (b, i, 0)), _mod_spec(layer, 1), _mod_spec(layer, 0)],
        out_specs=pl.BlockSpec((1, TM, D), lambda b, i: (b, i, 0)),
        out_shape=jax.ShapeDtypeStruct((NB, S, D), BF16),
        compiler_params=_params(2), name="modulate")(x, mod, mod)


def _swap32(x):
    n = x.shape[1]
    lane = lax.broadcasted_iota(I32, x.shape, 1)
    return jnp.where((lane & 32) == 0, pltpu.roll(x, n - 32, 1), pltpu.roll(x, 32, 1))


def _proj_kernel(a_ref, w_ref, cos_ref, sin_ref, o_ref, wbf_ref, *, rope_cols):
    @pl.when((pl.program_id(1) == 0) & (pl.program_id(2) == 0))
    def _():
        wbf_ref[...] = w_ref[...].astype(BF16)

    acc = jnp.dot(a_ref[0], wbf_ref[...], preferred_element_type=F32)
    if rope_cols:
        xr = acc[:, :rope_cols]
        reps = rope_cols // HD
        cos = jnp.concatenate([cos_ref[...]] * reps, axis=1)
        sin = jnp.concatenate([sin_ref[...]] * reps, axis=1)
        xr = xr * cos + _swap32(xr) * sin
        acc = xr if rope_cols == acc.shape[1] else jnp.concatenate([xr, acc[:, rope_cols:]], axis=1)
    o_ref[0] = acc.astype(o_ref.dtype)


def _proj(h, w, layer, col0, ncols, tn, rope_cols, rope, out_dtype, name):
    assert col0 % tn == 0 and ncols % tn == 0
    off = col0 // tn
    cos_t, sin_t = rope
    return pl.pallas_call(
        functools.partial(_proj_kernel, rope_cols=rope_cols),
        grid=(ncols // tn, NB, S // TMW),
        in_specs=[pl.BlockSpec((1, TMW, D), lambda j, b, i: (b, i, 0)),
                  pl.BlockSpec((None, D, tn), lambda j, b, i: (layer, 0, j + off)),
                  pl.BlockSpec((TMW, HD), lambda j, b, i: (i, 0)),
                  pl.BlockSpec((TMW, HD), lambda j, b, i: (i, 0))],
        out_specs=pl.BlockSpec((1, TMW, tn), lambda j, b, i: (b, i, j)),
        out_shape=jax.ShapeDtypeStruct((NB, S, ncols), out_dtype),
        scratch_shapes=[pltpu.VMEM((D, tn), BF16)],
        compiler_params=_params(3), name=name)(h, w, cos_t, sin_t)


def _rope_tables():
    t = jnp.arange(S)
    nf = HD // 4
    inv = ROPE_BASE ** (-2.0 * jnp.arange(nf, dtype=F32) / (HD // 2))
    lat = (t < L)[:, None]
    cos, sin = [], []
    for pos in (t // GW, t % GW):
        ang = pos.astype(F32)[:, None] * inv[None, :]
        c = jnp.where(lat, jnp.cos(ang), 1.0)
        s = jnp.where(lat, jnp.sin(ang), 0.0)
        cos += [c, c]
        sin += [-s, s]
    return jnp.concatenate(cos, axis=1), jnp.concatenate(sin, axis=1)


def _swa_kernel(sink_ref, q_ref, k0_ref, k1_ref, k2_ref, kc_ref, v0_ref, v1_ref, v2_ref, vc_ref, mask_ref, o_ref):
    mask = mask_ref[0]
    rg = lax.broadcasted_iota(I32, (SWA_G * SWA_W, 1), 0) // SWA_W
    for hk in range(SWA_KV):
        ks = slice(hk * HD, (hk + 1) * HD)
        qs = jnp.concatenate([q_ref[0, :, (hk * SWA_G + g) * HD:(hk * SWA_G + g + 1) * HD] for g in range(SWA_G)],
                             axis=0)
        k = jnp.concatenate([k0_ref[0, :, ks], k1_ref[0, :, ks], k2_ref[0, :, ks], kc_ref[0, :, ks]], axis=0)
        v = jnp.concatenate([v0_ref[0, :, ks], v1_ref[0, :, ks], v2_ref[0, :, ks], vc_ref[0, :, ks]], axis=0)
        s = lax.dot_general(qs, k, (((1,), (1,)), ((), ())), preferred_element_type=F32) * (HD ** -0.5 * LOG2E) + mask
        snk = jnp.zeros((SWA_G * SWA_W, 1), F32)
        for g in range(SWA_G):
            snk = jnp.where(rg == g, sink_ref[hk * SWA_G + g] * LOG2E, snk)
        m = jnp.maximum(jnp.max(s, axis=1, keepdims=True), snk)
        p = jnp.exp2(s - m)
        den = jnp.sum(p, axis=1, keepdims=True) + jnp.exp2(snk - m)
        o = jnp.dot(p.astype(BF16), v, preferred_element_type=F32) / den
        for g in range(SWA_G):
            o_ref[0, :, (hk * SWA_G + g) * HD:(hk * SWA_G + g + 1) * HD] = \
                o[g * SWA_W:(g + 1) * SWA_W].astype(o_ref.dtype)


def _swa_masks():
    r = jnp.arange(SWA_W)[:, None]
    c = jnp.arange(3 * SWA_W + LC)[None, :]
    band = (c - r >= 0) & (c - r <= 2 * SWA_W)
    is_ctx = c >= 3 * SWA_W
    variants = [band & (c >= SWA_W), band, band & (c < 2 * SWA_W), jnp.zeros_like(band)]
    m = jnp.stack([jnp.where(is_ctx | v, 0.0, NEG).astype(F32) for v in variants])
    return jnp.tile(m, (1, SWA_G, 1))


def _swa(q, kv, sink, masks):
    nb = L // SWA_W
    kvw = SWA_KV * HD

    def band(j, part):
        return pl.BlockSpec((1, SWA_W, kvw), lambda b, n: (b, jnp.clip(n + j - 1, 0, nb - 1), part))

    def ctx(part):
        return pl.BlockSpec((1, LC, kvw), lambda b, n: (b, L // LC, part))

    def variant(b, n):
        return (jnp.where(n == 0, 0, jnp.where(n < nb - 1, 1, jnp.where(n == nb - 1, 2, 3))), 0, 0)

    qspec = pl.BlockSpec((1, SWA_W, SWA_H * HD), lambda b, n: (b, n, 0))
    return pl.pallas_call(
        _swa_kernel, grid=(NB, S // SWA_W),
        in_specs=[pl.BlockSpec(memory_space=pltpu.SMEM), qspec,
                  band(0, 0), band(1, 0), band(2, 0), ctx(0), band(0, 1), band(1, 1), band(2, 1), ctx(1),
                  pl.BlockSpec((1,) + masks.shape[1:], variant)],
        out_specs=qspec,
        out_shape=jax.ShapeDtypeStruct((NB, S, SWA_H * HD), BF16),
        compiler_params=_params(2), name="swa")(sink, q, kv, kv, kv, kv, kv, kv, kv, kv, masks)


N_DR = 2 * NA_KH - 1
N_DC = 2 * NA_KW - 1
NA_G = 4
NA_WIN = NA_G + NA_KH
NA_QB = NA_G * GW
NA_KB = NA_WIN * GW // NA_QB


def _na_bias_kernel(rpb_ref, o_ref):
    h = pl.program_id(0)
    shape = (GW, 2 * GW)
    lane = lax.broadcasted_iota(I32, shape, 1)
    qc = lax.broadcasted_iota(I32, shape, 0)
    kc = lane & (GW - 1)
    diff = kc - qc + (NA_KW - 1)
    cs = jnp.clip(qc - NA_KW // 2, 0, GW - NA_KW)
    ok = (kc >= cs) & (kc < cs + NA_KW)
    right = lax.broadcasted_iota(I32, (1, 2 * GW), 1) >= GW
    for dd in range(N_DR + 1):
        acc = jnp.zeros(shape, F32)
        for m in range(N_DC):
            lo = rpb_ref[(h * N_DR + dd - 1) * N_DC + m] if dd >= 1 else 0.0
            hi = rpb_ref[(h * N_DR + dd) * N_DC + m] if dd < N_DR else 0.0
            acc = jnp.where(diff == m, jnp.where(right, hi, lo), acc)
        valid = ok & (right if dd == 0 else (~right if dd == N_DR else True))
        o_ref[0, dd] = jnp.where(valid, acc * LOG2E, NEG)


def _na_bias(rpb):
    return pl.pallas_call(
        _na_bias_kernel, grid=(NA_H,),
        in_specs=[pl.BlockSpec(memory_space=pltpu.SMEM)],
        out_specs=pl.BlockSpec((1, N_DR + 1, GW, 2 * GW), lambda h: (h, 0, 0, 0)),
        out_shape=jax.ShapeDtypeStruct((NA_H, N_DR + 1, GW, 2 * GW), F32),
        compiler_params=_params(1), name="na_bias")(rpb.reshape(-1))


def _na_kernel(*refs):
    q_ref = refs[0]
    k_refs = refs[1:1 + NA_KB]
    v_refs = refs[1 + NA_KB:1 + 2 * NA_KB]
    kc_ref, vc_ref, tz_ref, o_ref = refs[1 + 2 * NA_KB:]
    scale = HD ** -0.5 * LOG2E
    nt = (((1,), (1,)), ((), ()))
    rows = L // GW
    g = pl.program_id(1)
    is_lat = g < rows // NA_G
    r0 = g * NA_G
    ustart = jnp.clip(r0 - NA_KH // 2, 0, rows - NA_WIN)
    lane = lax.broadcasted_iota(I32, (1, NA_WIN * GW), 1)
    masks, dds = [], []
    for a in range(NA_G):
        r = r0 + a
        lo = (jnp.clip(r - NA_KH // 2, 0, rows - NA_KH) - ustart) * GW
        hi = jnp.where(is_lat, lo + NA_KH * GW, lo)
        masks.append((lane >= lo) & (lane < hi))
        dds.append([jnp.clip(ustart + 2 * p - r + NA_KH, 0, N_DR) for p in range(NA_WIN // 2)])
    for h in range(NA_H):
        hs = slice(h * HD, (h + 1) * HD)
        qh = q_ref[0, :, hs]
        kh = jnp.concatenate([r[0, :, hs] for r in k_refs], axis=0)
        vh = jnp.concatenate([r[0, :, hs] for r in v_refs], axis=0)
        s = lax.dot_general(qh, kh, nt, preferred_element_type=F32) * scale
        parts = []
        for a in range(NA_G):
            bias = jnp.concatenate([tz_ref[h, dd] for dd in dds[a]], axis=1)
            parts.append(jnp.where(masks[a], s[a * GW:(a + 1) * GW] + bias, NEG))
        s = jnp.concatenate(parts, axis=0)
        sc = lax.dot_general(qh, kc_ref[0, :, hs], nt, preferred_element_type=F32) * scale
        m = jnp.maximum(jnp.max(s, axis=1, keepdims=True), jnp.max(sc, axis=1, keepdims=True))
        p = jnp.exp2(s - m)
        pc = jnp.exp2(sc - m)
        den = jnp.sum(p, axis=1, keepdims=True) + jnp.sum(pc, axis=1, keepdims=True)
        o = jnp.dot(p.astype(BF16), vh, preferred_element_type=F32)
        o = o + jnp.dot(pc.astype(BF16), vc_ref[0, :, hs], preferred_element_type=F32)
        o_ref[0, :, hs] = (o / den).astype(o_ref.dtype)


def _na(qkv, tz):
    rows = L // GW
    assert LC == NA_QB and (rows - NA_WIN) % NA_G == 0

    def win(j, part):
        return pl.BlockSpec(
            (1, NA_QB, D), lambda b, g: (b, jnp.clip(g * NA_G - NA_KH // 2, 0, rows - NA_WIN) // NA_G + j, part))

    in_specs = [pl.BlockSpec((1, NA_QB, D), lambda b, g: (b, g, 0))]
    in_specs += [win(j, 1) for j in range(NA_KB)] + [win(j, 2) for j in range(NA_KB)]
    in_specs += [pl.BlockSpec((1, LC, D), lambda b, g: (b, L // LC, 1)),
                 pl.BlockSpec((1, LC, D), lambda b, g: (b, L // LC, 2)),
                 pl.BlockSpec(tz.shape, lambda b, g: (0, 0, 0, 0))]
    return pl.pallas_call(
        _na_kernel, grid=(NB, S // NA_QB),
        in_specs=in_specs,
        out_specs=pl.BlockSpec((1, NA_QB, D), lambda b, g: (b, g, 0)),
        out_shape=jax.ShapeDtypeStruct((NB, S, D), BF16),
        compiler_params=_params(2), name="na")(*([qkv] * (3 + 2 * NA_KB)), tz)


def _filter_kernel(z_ref, w1_ref, b1_ref, w2_ref, b2_ref, w3_ref, fr_ref, dl_ref, o_ref):
    hp = lax.Precision.HIGHEST
    z = z_ref[...]
    fr = fr_ref[...]
    h = jnp.sin(fr * (jnp.dot(z, w1_ref[...], precision=hp, preferred_element_type=F32) + b1_ref[...]))
    for i in range(w2_ref.shape[0]):
        h = jnp.sin(fr * (jnp.dot(h, w2_ref[i], precision=hp, preferred_element_type=F32) + b2_ref[i]))
    h = jnp.dot(h.astype(BF16), w3_ref[...].astype(BF16), preferred_element_type=F32)
    decay = jnp.exp(-z[:, 0:1] * dl_ref[...])
    parts = []
    for order in range(2):
        fwd = h[:, (2 * order) * HY:(2 * order + 1) * HY] * decay
        bwd = h[:, (2 * order + 1) * HY:(2 * order + 2) * HY] * decay
        parts += [fwd + bwd, fwd - bwd]
    o_ref[...] = jnp.concatenate(parts, axis=1).astype(o_ref.dtype)


def _filter_embedding(n):
    t = jnp.linspace(0.0, 1.0, n, dtype=F32)[:, None]
    w = (2.0 * math.pi / n) * jnp.arange(n, dtype=F32)[:, None]
    f = jnp.linspace(1e-4, HY_BANDS - 1, HY_BANDS, dtype=F32)[None, :]
    z = jnp.concatenate([t, jnp.cos(f * w), -jnp.sin(f * w)], axis=-1)
    return jnp.pad(z, ((0, 0), (0, HD - HY_EMB)))


def _hyena_filters(n, fw1, fb1, fw2, fb2, fw3, ffreq):
    deltas = jnp.abs(jnp.linspace(math.log(HY_TARGET) / HY_SLOW_DECAY, math.log(HY_TARGET) / HY_FAST_DECAY,
                                  HY, dtype=F32))[None, :]
    w1 = jnp.pad(fw1, ((0, HD - HY_EMB), (0, 0)))
    tm = min(n, 256)
    nout = fw3.shape[1]
    full = lambda a: pl.BlockSpec(a.shape, lambda i: (0,) * a.ndim)
    args = (_filter_embedding(n), w1, fb1.reshape(1, -1), fw2, fb2.reshape(fb2.shape[0], 1, -1), fw3,
            ffreq.reshape(1, -1), deltas)
    return pl.pallas_call(
        _filter_kernel, grid=(n // tm,),
        in_specs=[pl.BlockSpec((tm, HD), lambda i: (i, 0))] + [full(a) for a in args[1:]],
        out_specs=pl.BlockSpec((tm, nout), lambda i: (i, 0)),
        out_shape=jax.ShapeDtypeStruct((n, nout), BF16),
        compiler_params=_params(1), name="hy_filter")(*args)


def _shortconv_kernel(u_ref, w_ref, b_ref, o_ref):
    u = u_ref[0]
    t = lax.broadcasted_iota(I32, (S, 1), 0)
    prev = jnp.where((t == 0) | (t == L), 0.0, pltpu.roll(u, 1, 0))
    nxt = jnp.where((t == L - 1) | (t == S - 1), 0.0, pltpu.roll(u, S - 1, 0))
    w = w_ref[...]
    o_ref[0] = b_ref[...] + prev * w[0:1] + u * w[1:2] + nxt * w[2:3]


def _shortconv(u, w, b):
    tc = 256
    nc = u.shape[2]
    return pl.pallas_call(
        _shortconv_kernel, grid=(NB, nc // tc),
        in_specs=[pl.BlockSpec((1, S, tc), lambda b_, c: (b_, 0, c)),
                  pl.BlockSpec((3, tc), lambda b_, c: (0, c)),
                  pl.BlockSpec((1, tc), lambda b_, c: (0, c))],
        out_specs=pl.BlockSpec((1, S, tc), lambda b_, c: (b_, 0, c)),
        out_shape=jax.ShapeDtypeStruct(u.shape, F32),
        compiler_params=_params(2), name="shortconv")(u, w, b.reshape(1, -1))


def _dft_tables(n):
    nn = 2 * n
    lo = 64
    f = jnp.arange(n, dtype=I32)[:, None]
    angle = lambda m: (m % nn).astype(F32) * (2.0 * math.pi / nn)
    a = angle(f * (jnp.arange(n // lo, dtype=I32) * lo)[None, :])[:, :, None]
    b = angle(f * jnp.arange(lo, dtype=I32)[None, :])[:, None, :]
    c = (jnp.cos(a) * jnp.cos(b) - jnp.sin(a) * jnp.sin(b)).reshape(n, n)
    s = -(jnp.sin(a) * jnp.cos(b) + jnp.cos(a) * jnp.sin(b)).reshape(n, n)
    i = jnp.arange(n, dtype=I32)
    alt = jnp.where(i % 2 == 0, 1.0, -1.0).astype(F32)
    fwd = jnp.concatenate([c, jnp.where(i[:, None] == 0, alt[None, :], s)], axis=0)
    inv_r = c * jnp.where(i == 0, 1.0 / nn, 2.0 / nn).astype(F32)[None, :]
    inv_i = jnp.where(i[None, :] == 0, alt[:, None] / nn, s * (2.0 / nn))
    return fwd.astype(BF16), inv_r.astype(BF16), inv_i.astype(BF16)


HY_TN = 512
HY_NCT = HY // HY_TN


def _filt_dft_kernel(c_ref, s_ref, hs_ref, hd_ref, kr_ref, ki_ref):
    kr_ref[0] = jnp.dot(c_ref[...], hs_ref[...], preferred_element_type=F32)
    ki_ref[0] = jnp.dot(s_ref[...], hd_ref[...], preferred_element_type=F32)

    @pl.when(pl.program_id(0) == 0)
    def _():
        nyq = jnp.dot(s_ref[0:8, :], hs_ref[...], preferred_element_type=F32)
        first = lax.broadcasted_iota(I32, nyq.shape, 0) == 0
        ki_ref[0, 0:8, :] = jnp.where(first, nyq, ki_ref[0, 0:8, :])


def _filt_dft(fwd, hfilt, n):
    tf = min(n, 1024)
    nf = n // tf
    out = pl.BlockSpec((1, tf, HY_TN), lambda f, c: (c // HY_NCT, f, c % HY_NCT))
    return pl.pallas_call(
        _filt_dft_kernel, grid=(nf, 2 * HY_NCT),
        in_specs=[pl.BlockSpec((tf, n), lambda f, c: (f, 0)),
                  pl.BlockSpec((tf, n), lambda f, c: (nf + f, 0)),
                  pl.BlockSpec((n, HY_TN), lambda f, c: (0, (c // HY_NCT) * 2 * HY_NCT + c % HY_NCT)),
                  pl.BlockSpec((n, HY_TN), lambda f, c: (0, (c // HY_NCT) * 2 * HY_NCT + HY_NCT + c % HY_NCT))],
        out_specs=[out, out],
        out_shape=[jax.ShapeDtypeStruct((2, n, HY), F32)] * 2,
        compiler_params=_params(2), name=f"filt_dft{n}")(fwd, fwd, hfilt, hfilt)


def _dft_mul_kernel(z_ref, c_ref, s_ref, kr_ref, ki_ref, yr_ref, yi_ref):
    z = z_ref[0].astype(BF16)
    zr = jnp.dot(c_ref[...], z, preferred_element_type=F32)
    zi = jnp.dot(s_ref[...], z, preferred_element_type=F32)
    kr = kr_ref[...]
    ki = ki_ref[...]
    first = (lax.broadcasted_iota(I32, zr.shape, 0) == 0) & (pl.program_id(0) == 0)
    ii = zi * ki
    yr_ref[0] = (zr * kr - jnp.where(first, 0.0, ii)).astype(yr_ref.dtype)
    yi_ref[0] = jnp.where(first, ii, zr * ki + zi * kr).astype(yi_ref.dtype)


def _dft_mul(z, zrow, zc0, n, fwd, kr, ki, order):
    tf = min(n, 1024)
    nf = n // tf
    out = pl.BlockSpec((1, tf, HY_TN), lambda f, b, c: (b, f, c))
    kspec = lambda: pl.BlockSpec((None, tf, HY_TN), lambda f, b, c: (order, f, c))
    return pl.pallas_call(
        _dft_mul_kernel, grid=(nf, NB, HY_NCT),
        in_specs=[pl.BlockSpec((1, n, HY_TN), lambda f, b, c: (b, zrow, zc0 + c)),
                  pl.BlockSpec((tf, n), lambda f, b, c: (f, 0)),
                  pl.BlockSpec((tf, n), lambda f, b, c: (nf + f, 0)),
                  kspec(), kspec()],
        out_specs=[out, out],
        out_shape=[jax.ShapeDtypeStruct((NB, n, HY), BF16)] * 2,
        compiler_params=_params(3), name=f"dft_mul{n}")(z, fwd, fwd, kr, ki)


def _idft_gate_kernel(ar_ref, ai_ref, yr_ref, yi_ref, z_ref, p_ref, bias_ref, *o_refs):
    y = jnp.dot(ar_ref[...], yr_ref[0], preferred_element_type=F32)
    y = y + jnp.dot(ai_ref[...], yi_ref[0], preferred_element_type=F32)
    out = p_ref[0] * (y + z_ref[0] * bias_ref[0])
    for o_ref in o_refs:
        o_ref[0] = out.astype(o_ref.dtype)


def _idft_gate(yr, yi, inv_r, inv_i, n, z, z_r0, z_c0, p, p_r0, p_c0, bias, order, out_dtype):
    tm = min(n, 1024)
    return pl.pallas_call(
        _idft_gate_kernel, grid=(n // tm, NB, HY_NCT),
        in_specs=[pl.BlockSpec((tm, n), lambda t, b, c: (t, 0)),
                  pl.BlockSpec((tm, n), lambda t, b, c: (t, 0)),
                  pl.BlockSpec((1, n, HY_TN), lambda t, b, c: (b, 0, c)),
                  pl.BlockSpec((1, n, HY_TN), lambda t, b, c: (b, 0, c)),
                  pl.BlockSpec((1, tm, HY_TN), lambda t, b, c: (b, z_r0 + t, z_c0 + c)),
                  pl.BlockSpec((1, tm, HY_TN), lambda t, b, c: (b, p_r0 + t, p_c0 + c)),
                  pl.BlockSpec((1, 1, HY_TN), lambda t, b, c: (order, 0, c))],
        out_specs=pl.BlockSpec((1, tm, HY_TN), lambda t, b, c: (b, t, c)),
        out_shape=jax.ShapeDtypeStruct((NB, n, HY), out_dtype),
        compiler_params=_params(3), name=f"idft_gate{n}_{order}")(inv_r, inv_i, yr, yi, z, p, bias)


def _hyena(u, conv_w, conv_b, filt, hy_bias, tables):
    sc = _shortconv(u, conv_w, conv_b)
    bias = hy_bias.reshape(2, 1, HY)
    outs = []
    for n in (L, LC):
        zrow = 0 if n == L else L // LC
        r0 = zrow * (n // min(n, 1024))
        fwd, inv_r, inv_i = tables[n]
        kr, ki = _filt_dft(fwd, _hyena_filters(n, *filt), n)
        yr, yi = _dft_mul(sc, zrow, 2 * HY_NCT, n, fwd, kr, ki, 0)
        z1 = _idft_gate(yr, yi, inv_r, inv_i, n, sc, r0, 2 * HY_NCT, sc, r0, 0, bias, 0, F32)
        yr, yi = _dft_mul(z1, 0, 0, n, fwd, kr, ki, 1)
        outs.append(_idft_gate(yr, yi, inv_r, inv_i, n, z1, 0, 0, sc, r0, HY_NCT, bias, 1, BF16))
    return outs


def _resid_ln(x, y, gate, g, b):
    v = DN_ALPHA * x + gate * y
    mu = jnp.mean(v, axis=-1, keepdims=True)
    vc = v - mu
    var = jnp.mean(vc * vc, axis=-1, keepdims=True)
    return vc * lax.rsqrt(var + LN_EPS) * g + b


def _wout_ln_kernel(a1a_ref, a1b_ref, a1c_ref, a1x_ref, a2_ref, w_ref, x_ref, gl_ref, gc_ref, g_ref, b_ref,
                    scl_ref, scc_ref, shl_ref, shc_ref, wrh_ref, wrl_ref, xo_ref, hm_ref, aff_ref):
    half = a2_ref.shape[2]
    last_tile = pl.program_id(1) == pl.num_programs(1) - 1
    a1_refs = (a1a_ref, a1b_ref, a1c_ref)
    n_sub = len(a1_refs)

    def project(s):
        a1 = jnp.where(last_tile, a1x_ref[0], a1_refs[s][0]) if s == n_sub - 1 else a1_refs[s][0]
        y = jnp.dot(a1, w_ref[0:half, :], preferred_element_type=F32)
        return y + jnp.dot(a2_ref[0, s * TM:(s + 1) * TM, :], w_ref[half:2 * half, :], preferred_element_type=F32)

    for s in range(n_sub):
        rows = slice(s * TM, (s + 1) * TM)
        if s == n_sub - 1:
            pick = lambda lat_ref, ctx_ref: jnp.where(last_tile, ctx_ref[0], lat_ref[0])
        else:
            pick = lambda lat_ref, ctx_ref: lat_ref[0]
        xn = _resid_ln(x_ref[0, rows, :], project(s), pick(gl_ref, gc_ref), g_ref[0], b_ref[0])
        xo_ref[0, rows, :] = xn
        hm = xn * (1.0 + pick(scl_ref, scc_ref)) + pick(shl_ref, shc_ref)
        hm_hi = hm.astype(BF16)
        hm_ref[0, rows, :] = hm_hi
        hm_lo = (hm - hm_hi.astype(F32)).astype(BF16)
        logits = jnp.dot(hm_hi, wrh_ref[...], preferred_element_type=F32)
        logits = logits + jnp.dot(hm_lo, wrh_ref[...], preferred_element_type=F32)
        logits = logits + jnp.dot(hm_hi, wrl_ref[...], preferred_element_type=F32)
        e = jnp.exp(logits - jnp.max(logits, axis=-1, keepdims=True))
        aff_ref[0, rows, :] = e / jnp.sum(e, axis=-1, keepdims=True)


def _wout_ln(a1_lat, a1_ctx, a1_ctx_blk, c1, a2, c2, w_bf, x, mod, layer, lng, lnb, w_router):
    half = D // 2
    sub = TMW // TM
    n_lat_blk = a1_lat.shape[1] // TM
    tile = lambda: pl.BlockSpec((1, TMW, D), lambda b, i: (b, i, 0))
    vec = lambda idx: pl.BlockSpec((1, 1, D), idx)
    lat = lambda k: vec(lambda b, i: ((layer * 8 + b) * 6 + k, 0, 0))
    ctx = lambda k: vec(lambda b, i: ((layer * 8 + NB) * 6 + k, 0, 0))
    a1_blk = lambda s: pl.BlockSpec((1, TM, half), lambda b, i: (b, jnp.minimum(i * sub + s, n_lat_blk - 1), c1))
    wr_hi = w_router.astype(BF16)
    wr_lo = (w_router - wr_hi.astype(F32)).astype(BF16)
    return pl.pallas_call(
        _wout_ln_kernel, grid=(NB, S // TMW),
        in_specs=[a1_blk(0), a1_blk(1), a1_blk(2),
                  pl.BlockSpec((1, TM, half), lambda b, i: (b, a1_ctx_blk, c1)),
                  pl.BlockSpec((1, TMW, half), lambda b, i: (b, i, c2)),
                  pl.BlockSpec((D, D), lambda b, i: (0, 0), pipeline_mode=pl.Buffered(1)),
                  tile(), lat(2), ctx(2),
                  vec(lambda b, i: (2 * layer, 0, 0)), vec(lambda b, i: (2 * layer, 0, 0)),
                  lat(4), ctx(4), lat(3), ctx(3),
                  pl.BlockSpec((D, NE), lambda b, i: (0, 0)),
                  pl.BlockSpec((D, NE), lambda b, i: (0, 0))],
        out_specs=[tile(), tile(), pl.BlockSpec((1, TMW, NE), lambda b, i: (b, i, 0))],
        out_shape=[jax.ShapeDtypeStruct((NB, S, D), F32), jax.ShapeDtypeStruct((NB, S, D), BF16),
                   jax.ShapeDtypeStruct((NB, S, NE), F32)],
        compiler_params=_params(2), name="wout_ln")(
            a1_lat, a1_lat, a1_lat, a1_ctx, a2, w_bf, x, mod, mod, lng, lnb, mod, mod, mod, mod, wr_hi, wr_lo)


def _route_kernel(a_ref, o_ref, tri_ref):
    rb = 256
    for r in range(L // rb):
        s_i = lax.broadcasted_iota(I32, (rb, L), 0) + r * rb
        t_i = lax.broadcasted_iota(I32, (rb, L), 1)
        tri_ref[r * rb:(r + 1) * rb, :] = jnp.where(s_i < t_i, 1.0, 0.0).astype(BF16)

    def select(a, cap, base):
        n = a.shape[1]

        def body(i, prefix):
            cand = prefix | jnp.left_shift(jnp.int32(1), 30 - i)
            cnt = jnp.sum(jnp.where(a >= lax.bitcast_convert_type(cand, F32), 1.0, 0.0), axis=1, keepdims=True)
            return jnp.where(cnt >= cap, cand, prefix)

        kth = lax.bitcast_convert_type(lax.fori_loop(0, 31, body, jnp.zeros((a.shape[0], 1), I32)), F32)
        gt = a > kth
        eq = a == kth
        n_gt = jnp.sum(jnp.where(gt, 1.0, 0.0), axis=1, keepdims=True)
        tri = tri_ref[0:n, 0:n]
        before = jnp.dot(jnp.where(eq, 1.0, 0.0).astype(BF16), tri, preferred_element_type=F32)
        sel = gt | (eq & (before < cap - n_gt))
        pos = jnp.dot(jnp.where(sel, 1.0, 0.0).astype(BF16), tri, preferred_element_type=F32)
        return jnp.where(sel, pos.astype(I32) + base, -1)

    a = a_ref[...]
    o_ref[...] = jnp.concatenate([select(a[:, :L], CAP_L, 0), select(a[:, L:], CAP_C, CAP_L)], axis=1)


def _route(aff_t):
    return pl.pallas_call(
        _route_kernel, grid=(1,),
        in_specs=[pl.BlockSpec((NB * NE, S), lambda i: (0, 0))],
        out_specs=pl.BlockSpec((NB * NE, S), lambda i: (0, 0)),
        out_shape=jax.ShapeDtypeStruct((NB * NE, S), I32),
        scratch_shapes=[pltpu.VMEM((L, L), BF16)],
        compiler_params=_params(1), name="route")(aff_t.reshape(NB * NE, S)).reshape(NB, NE, S)


GATHER_EG = 4


def _gather_kernel(slot_ref, aff_ref, h_ref, xs_ref, g_ref):
    for r0, cap, c0, n in ((0, CAP_L, 0, L), (CAP_L, CAP_C, L, LC)):
        want = lax.broadcasted_iota(I32, (cap, n), 0) + r0
        hits = []
        for e in range(GATHER_EG):
            hit = slot_ref[0, e][:, c0:c0 + n] == want
            hits.append(jnp.where(hit, 1.0, 0.0).astype(BF16))
            g = jnp.sum(jnp.where(hit, aff_ref[0, e][:, c0:c0 + n], 0.0), axis=1, keepdims=True)
            g_ref[0, e, r0:r0 + cap, :] = jnp.broadcast_to(g, (cap, HD))
        xs = jnp.dot(jnp.concatenate(hits, axis=0), h_ref[0, c0:c0 + n, :], preferred_element_type=F32)
        for e in range(GATHER_EG):
            xs_ref[0, e, r0:r0 + cap, :] = xs[e * cap:(e + 1) * cap].astype(xs_ref.dtype)


def _gather(slot, aff_t, hm):
    row = lambda: pl.BlockSpec((1, GATHER_EG, 1, S), lambda b, e: (b, e, 0, 0))
    return pl.pallas_call(
        _gather_kernel, grid=(NB, NE // GATHER_EG),
        in_specs=[row(), row(), pl.BlockSpec((1, S, D), lambda b, e: (b, 0, 0))],
        out_specs=[pl.BlockSpec((1, GATHER_EG, CAP, D), lambda b, e: (b, e, 0, 0)),
                   pl.BlockSpec((1, GATHER_EG, CAP, HD), lambda b, e: (b, e, 0, 0))],
        out_shape=[jax.ShapeDtypeStruct((NB, NE, CAP, D), BF16), jax.ShapeDtypeStruct((NB, NE, CAP, HD), F32)],
        compiler_params=_params(2), name="moe_gather")(
            slot.reshape(NB, NE, 1, S), aff_t.reshape(NB, NE, 1, S), hm)


FFN_SPLIT = 4


def _ffn_kernel(xs_ref, g_ref, w1_ref, w3_ref, w2_ref, ol_ref, oc_ref, hmid_ref):
    f = pl.program_id(1)
    tf = FF // FFN_SPLIT

    @pl.when(f < FFN_SPLIT)
    def _():
        x = xs_ref[:, 0].reshape(NB * CAP, D)
        a = jnp.dot(x, w1_ref[0].astype(BF16), preferred_element_type=F32)
        u = jnp.dot(x, w3_ref[0].astype(BF16), preferred_element_type=F32)
        hmid_ref[:, pl.ds(pl.multiple_of(f * tf, tf), tf)] = (a * jax.nn.sigmoid(a) * u).astype(BF16)

    @pl.when(f >= FFN_SPLIT)
    def _():
        y = jnp.dot(hmid_ref[...], w2_ref[0].astype(BF16), preferred_element_type=F32)
        g = g_ref[:, 0].reshape(NB * CAP, HD)[:, 0:1]
        y = (y * g).astype(ol_ref.dtype).reshape(NB, CAP, y.shape[1])
        ol_ref[:, 0] = y[:, :CAP_L]
        oc_ref[:, 0] = y[:, CAP_L:]


def _ffn(xs, gate, w1, w3, w2, layer):
    tf = FF // FFN_SPLIT
    tn = D // FFN_SPLIT
    blk = lambda width: pl.BlockSpec((NB, 1, CAP, width), lambda e, f: (0, e, 0, 0))
    hid = lambda e, f: (layer, e, 0, jnp.minimum(f, FFN_SPLIT - 1))
    col = lambda f: jnp.maximum(f - FFN_SPLIT, 0)
    out = lambda cap: pl.BlockSpec((NB, 1, cap, tn), lambda e, f: (0, e, 0, col(f)))
    return pl.pallas_call(
        _ffn_kernel, grid=(NE, 2 * FFN_SPLIT),
        in_specs=[blk(D), blk(HD),
                  pl.BlockSpec((None, 1, D, tf), hid),
                  pl.BlockSpec((None, 1, D, tf), hid),
                  pl.BlockSpec((None, 1, FF, tn), lambda e, f: (layer, e, 0, col(f)))],
        out_specs=[out(CAP_L), out(CAP_C)],
        out_shape=[jax.ShapeDtypeStruct((NB, NE, CAP_L, D), BF16), jax.ShapeDtypeStruct((NB, NE, CAP_C, D), BF16)],
        scratch_shapes=[pltpu.VMEM((NB * CAP, FF), BF16)],
        compiler_params=_params(2), name="moe_ffn")(xs, gate, w1, w3, w2)


def _combine_ln_kernel(*refs, last):
    if last:
        slot_ref, yl_ref, x_ref, gate_ref, g_ref, b_ref, xo_ref, acc_ref = refs
    else:
        slot_ref, yl_ref, yc_ref, x_ref, gate_ref, g_ref, b_ref, sc_ref, sh_ref, xo_ref, hn_ref, acc_ref = refs
    st = slot_ref[0]

    def scatter(y_ref, r0, cap):
        want = lax.broadcasted_iota(I32, (TM, cap), 1) + r0
        hits = [jnp.where(st[:, e:e + 1] == want, 1.0, 0.0).astype(BF16) for e in range(NE)]
        if cap % HD == 0:
            acc_ref[...] = jnp.dot(jnp.concatenate(hits, axis=1), y_ref[0].reshape(NE * cap, D),
                                   preferred_element_type=F32)
        else:
            acc = None
            for e in range(NE):
                part = jnp.dot(hits[e], y_ref[0, e], preferred_element_type=F32)
                acc = part if acc is None else acc + part
            acc_ref[...] = acc

    if last:
        scatter(yl_ref, 0, CAP_L)
    else:
        is_lat = pl.program_id(1) < N_LAT_TILES
        pl.when(is_lat)(lambda: scatter(yl_ref, 0, CAP_L))
        pl.when(jnp.logical_not(is_lat))(lambda: scatter(yc_ref, CAP_L, CAP_C))
    xn = _resid_ln(x_ref[0], acc_ref[...], gate_ref[0], g_ref[0], b_ref[0])
    xo_ref[0] = xn
    if not last:
        hn_ref[0] = (xn * (1.0 + sc_ref[0]) + sh_ref[0]).astype(hn_ref.dtype)


def _combine_ln(slot_t, yl, yc, x, mod, layer, lng, lnb, last):
    rows = L if last else S
    tile = lambda: pl.BlockSpec((1, TM, D), lambda b, i: (b, i, 0))
    ln_vec = lambda: pl.BlockSpec((1, 1, D), lambda b, i: (2 * layer + 1, 0, 0))
    in_specs = [pl.BlockSpec((1, TM, NE), lambda b, i: (b, i, 0)),
                pl.BlockSpec((1, NE, CAP_L, D), lambda b, i: (b, 0, 0, 0), pipeline_mode=pl.Buffered(1))]
    args = [slot_t, yl]
    if not last:
        in_specs.append(pl.BlockSpec((1, NE, CAP_C, D), lambda b, i: (b, 0, 0, 0)))
        args.append(yc)
    in_specs += [tile(), _mod_spec(layer, 5), ln_vec(), ln_vec()]
    args += [x, mod, lng, lnb]
    out_specs = [tile()]
    out_shape = [jax.ShapeDtypeStruct((NB, rows, D), F32)]
    if not last:
        in_specs += [_mod_spec(layer + 1, 1), _mod_spec(layer + 1, 0)]
        args += [mod, mod]
        out_specs.append(tile())
        out_shape.append(jax.ShapeDtypeStruct((NB, rows, D), BF16))
    return pl.pallas_call(
        functools.partial(_combine_ln_kernel, last=last), grid=(NB, rows // TM),
        in_specs=in_specs, out_specs=out_specs, out_shape=out_shape,
        scratch_shapes=[pltpu.VMEM((TM, D), F32)],
        compiler_params=_params(2), name="moe_combine_ln")(*args)


def _moe(hm, aff, x, mod, layer, lng, lnb, w1, w3, w2, last):
    aff_t = jnp.swapaxes(aff, 1, 2)
    slot = _route(aff_t)
    xs, gate = _gather(slot, aff_t, hm)
    yl, yc = _ffn(xs, gate, w1, w3, w2, layer)
    return _combine_ln(jnp.swapaxes(slot, 1, 2), yl, yc, x, mod, layer, lng, lnb, last)


def kernel(x, c, ctx, c_ctx, ada_w, ada_b, ln_g, ln_b, ev_w_in, ev_w_out, hy_conv_w, hy_conv_b, hy_f_w1, hy_f_b1,
           hy_f_w2, hy_f_b2, hy_f_w3, hy_f_freq, hy_bias, swa_sink, od_w_in, od_w_out, na_rpb, moe_w_router,
           moe_w1, moe_w3, moe_w2):
    assert x.shape == (NB, L, D) and ctx.shape == (NB, LC, D)
    xa = jnp.concatenate([x, ctx], axis=1)
    cc = jnp.concatenate([c, c_ctx[None, :], jnp.zeros((8 - NB - 1, D), F32)], axis=0)
    mod = _ada(cc, ada_w, ada_b).reshape(DEPTH * 8 * 6, 1, D)
    lng = ln_g.reshape(DEPTH * 2, 1, D)
    lnb = ln_b.reshape(DEPTH * 2, 1, D)
    rope = _rope_tables()
    tables = {L: _dft_tables(L), LC: _dft_tables(LC)}
    swa_masks = _swa_masks()
    h = _modulate(xa, mod, 0)
    for l in range(DEPTH):
        i = l // 2
        last = l == DEPTH - 1
        if l % 2 == 0:
            u = _proj(h, ev_w_in, i, 0, EV_Q0, 1024, 0, rope, F32, "proj_hy")
            q = _proj(h, ev_w_in, i, EV_Q0, SWA_H * HD, 1024, SWA_H * HD, rope, BF16, "proj_q")
            kv = _proj(h, ev_w_in, i, EV_K0, 2 * SWA_KV * HD, 512, SWA_KV * HD, rope, BF16, "proj_kv")
            att = _swa(q, kv, swa_sink[i], swa_masks)
            filt = (hy_f_w1[i], hy_f_b1[i], hy_f_w2[i], hy_f_b2[i], hy_f_w3[i], hy_f_freq[i])
            z_lat, z_ctx = _hyena(u, hy_conv_w[i], hy_conv_b[i], filt, hy_bias[i], tables)
            xa, hm, aff = _wout_ln(z_lat, z_ctx, 0, 0, att, 0, ev_w_out[i].astype(BF16), xa, mod, l, lng, lnb,
                                   moe_w_router[l])
        else:
            qkv = _proj(h, od_w_in, i, 0, 3 * D, 1024, 0, rope, BF16, "proj_qkv")
            att = _na(qkv, _na_bias(na_rpb[i]))
            xa, hm, aff = _wout_ln(att, att, L // TM, 0, att, 1, od_w_out[i].astype(BF16), xa, mod, l, lng, lnb,
                                   moe_w_router[l])
        out = _moe(hm, aff, xa, mod, l, lng, lnb, moe_w1, moe_w3, moe_w2, last)
        if last:
            return out[0]
        xa, h = out
```

```python
import functools
import math

import jax
import jax.numpy as jnp
from jax import lax
from jax.experimental import pallas as pl
from jax.experimental.pallas import tpu as pltpu

F32 = jnp.float32
BF16 = jnp.bfloat16
I32 = jnp.int32

D = 2048
NB = 4
L = 2048
LC = 256
S = L + LC
DEPTH = 4
GW = 64
HD = 128
HY = D // 2
HY_BANDS = 16
HY_EMB = 2 * HY_BANDS + 1
HY_HID = 64
HY_FAST_DECAY = 0.3
HY_SLOW_DECAY = 1.5
HY_TARGET = 1e-2
SWA_H = (D - HY) // HD
SWA_KV = SWA_H // 4
SWA_G = SWA_H // SWA_KV
SWA_W = 128
NA_H = D // HD
NA_KH = 8
NA_KW = 16
ROPE_BASE = 10000.0
NE = 16
CAP_L = 2 * L // NE
CAP_C = 2 * LC // NE
CAP = CAP_L + CAP_C
FF = D // 2
DN_ALPHA = (2 * DEPTH) ** 0.25
LN_EPS = 1e-5
EV_Q0 = 3 * HY
EV_K0 = EV_Q0 + SWA_H * HD
EV_IN = EV_K0 + 2 * SWA_KV * HD
NEG = -1e30
LOG2E = math.log2(math.e)

TM = 256
N_LAT_TILES = L // TM
TMW = 3 * TM
VMEM_LIMIT = 56 * 1024 * 1024


def _params(n_axes, vmem=VMEM_LIMIT):
    return pltpu.CompilerParams(dimension_semantics=("arbitrary",) * n_axes, vmem_limit_bytes=vmem)


def _mod_spec(layer, k):
    def index(b, i):
        row = jnp.where(i >= N_LAT_TILES, NB, b)
        return ((layer * 8 + row) * 6 + k, 0, 0)
    return pl.BlockSpec((1, 1, D), index)


def _ada_kernel(c_ref, w_ref, b_ref, o_ref):
    c = c_ref[...]
    s = c * jax.nn.sigmoid(c)
    o_ref[0] = jnp.dot(s.astype(BF16), w_ref[0].astype(BF16), preferred_element_type=F32) + b_ref[0]


def _ada(cc, ada_w, ada_b):
    tn = 1024
    return pl.pallas_call(
        _ada_kernel, grid=(DEPTH, 6 * D // tn),
        in_specs=[pl.BlockSpec((8, D), lambda l, j: (0, 0)),
                  pl.BlockSpec((1, D, tn), lambda l, j: (l, 0, j)),
                  pl.BlockSpec((1, 1, tn), lambda l, j: (l, 0, j))],
        out_specs=pl.BlockSpec((1, 8, tn), lambda l, j: (l, 0, j)),
        out_shape=jax.ShapeDtypeStruct((DEPTH, 8, 6 * D), F32),
        compiler_params=_params(2), name="ada")(cc, ada_w, ada_b.reshape(DEPTH, 1, 6 * D))


def _mod_kernel(x_ref, sc_ref, sh_ref, o_ref):
    o_ref[0] = (x_ref[0] * (1.0 + sc_ref[0]) + sh_ref[0]).astype(o_ref.dtype)


def _modulate(x, mod, layer):
    return pl.pallas_call(
        _mod_kernel, grid=(NB, S // TM),
        in_specs=[pl.BlockSpec((1, TM, D), lambda b, i: (b, i, 0)), _mod_spec(layer, 1), _mod_spec(layer, 0)],
        out_specs=pl.BlockSpec((1, TM, D), lambda b, i: (b, i, 0)),
        out_shape=jax.ShapeDtypeStruct((NB, S, D), BF16),
        compiler_params=_params(2), name="modulate")(x, mod, mod)


def _swap32(x):
    n = x.shape[1]
    lane = lax.broadcasted_iota(I32, x.shape, 1)
    return jnp.where((lane & 32) == 0, pltpu.roll(x, n - 32, 1), pltpu.roll(x, 32, 1))


def _proj_kernel(a_ref, w_ref, cos_ref, sin_ref, o_ref, wbf_ref, *, rope_cols):
    @pl.when((pl.program_id(1) == 0) & (pl.program_id(2) == 0))
    def _():
        wbf_ref[...] = w_ref[...].astype(BF16)

    acc = jnp.dot(a_ref[0], wbf_ref[...], preferred_element_type=F32)
    if rope_cols:
        xr = acc[:, :rope_cols]
        reps = rope_cols // HD
        cos = jnp.concatenate([cos_ref[...]] * reps, axis=1)
        sin = jnp.concatenate([sin_ref[...]] * reps, axis=1)
        xr = xr * cos + _swap32(xr) * sin
        acc = xr if rope_cols == acc.shape[1] else jnp.concatenate([xr, acc[:, rope_cols:]], axis=1)
    o_ref[0] = acc.astype(o_ref.dtype)


def _proj(h, w, layer, col0, ncols, tn, rope_cols, rope, out_dtype, name):
    assert col0 % tn == 0 and ncols % tn == 0
    off = col0 // tn
    cos_t, sin_t = rope
    return pl.pallas_call(
        functools.partial(_proj_kernel, rope_cols=rope_cols),
        grid=(ncols // tn, NB, S // TMW),
        in_specs=[pl.BlockSpec((1, TMW, D), lambda j, b, i: (b, i, 0)),
                  pl.BlockSpec((None, D, tn), lambda j, b, i: (layer, 0, j + off)),
                  pl.BlockSpec((TMW, HD), lambda j, b, i: (i, 0)),
                  pl.BlockSpec((TMW, HD), lambda j, b, i: (i, 0))],
        out_specs=pl.BlockSpec((1, TMW, tn), lambda j, b, i: (b, i, j)),
        out_shape=jax.ShapeDtypeStruct((NB, S, ncols), out_dtype),
        scratch_shapes=[pltpu.VMEM((D, tn), BF16)],
        compiler_params=_params(3), name=name)(h, w, cos_t, sin_t)


def _rope_tables():
    t = jnp.arange(S)
    nf = HD // 4
    inv = ROPE_BASE ** (-2.0 * jnp.arange(nf, dtype=F32) / (HD // 2))
    lat = (t < L)[:, None]
    cos, sin = [], []
    for pos in (t // GW, t % GW):
        ang = pos.astype(F32)[:, None] * inv[None, :]
        c = jnp.where(lat, jnp.cos(ang), 1.0)
        s = jnp.where(lat, jnp.sin(ang), 0.0)
        cos += [c, c]
        sin += [-s, s]
    return jnp.concatenate(cos, axis=1), jnp.concatenate(sin, axis=1)


def _swa_kernel(sink_ref, q_ref, k0_ref, k1_ref, k2_ref, kc_ref, v0_ref, v1_ref, v2_ref, vc_ref, mask_ref, o_ref):
    mask = mask_ref[0]
    rg = lax.broadcasted_iota(I32, (SWA_G * SWA_W, 1), 0) // SWA_W
    for hk in range(SWA_KV):
        ks = slice(hk * HD, (hk + 1) * HD)
        qs = jnp.concatenate([q_ref[0, :, (hk * SWA_G + g) * HD:(hk * SWA_G + g + 1) * HD] for g in range(SWA_G)],
                             axis=0)
        k = jnp.concatenate([k0_ref[0, :, ks], k1_ref[0, :, ks], k2_ref[0, :, ks], kc_ref[0, :, ks]], axis=0)
        v = jnp.concatenate([v0_ref[0, :, ks], v1_ref[0, :, ks], v2_ref[0, :, ks], vc_ref[0, :, ks]], axis=0)
        s = lax.dot_general(qs, k, (((1,), (1,)), ((), ())), preferred_element_type=F32) * (HD ** -0.5 * LOG2E) + mask
        snk = jnp.zeros((SWA_G * SWA_W, 1), F32)
        for g in range(SWA_G):
            snk = jnp.where(rg == g, sink_ref[hk * SWA_G + g] * LOG2E, snk)
        m = jnp.maximum(jnp.max(s, axis=1, keepdims=True), snk)
        p = jnp.exp2(s - m)
        den = jnp.sum(p, axis=1, keepdims=True) + jnp.exp2(snk - m)
        o = jnp.dot(p.astype(BF16), v, preferred_element_type=F32) / den
        for g in range(SWA_G):
            o_ref[0, :, (hk * SWA_G + g) * HD:(hk * SWA_G + g + 1) * HD] = \
                o[g * SWA_W:(g + 1) * SWA_W].astype(o_ref.dtype)


def _swa_masks():
    r = jnp.arange(SWA_W)[:, None]
    c = jnp.arange(3 * SWA_W + LC)[None, :]
    band = (c - r >= 0) & (c - r <= 2 * SWA_W)
    is_ctx = c >= 3 * SWA_W
    variants = [band & (c >= SWA_W), band, band & (c < 2 * SWA_W), jnp.zeros_like(band)]
    m = jnp.stack([jnp.where(is_ctx | v, 0.0, NEG).astype(F32) for v in variants])
    return jnp.tile(m, (1, SWA_G, 1))


def _swa(q, kv, sink, masks):
    nb = L // SWA_W
    kvw = SWA_KV * HD

    def band(j, part):
        return pl.BlockSpec((1, SWA_W, kvw), lambda b, n: (b, jnp.clip(n + j - 1, 0, nb - 1), part))

    def ctx(part):
        return pl.BlockSpec((1, LC, kvw), lambda b, n: (b, L // LC, part))

    def variant(b, n):
        return (jnp.where(n == 0, 0, jnp.where(n < nb - 1, 1, jnp.where(n == nb - 1, 2, 3))), 0, 0)

    qspec = pl.BlockSpec((1, SWA_W, SWA_H * HD), lambda b, n: (b, n, 0))
    return pl.pallas_call(
        _swa_kernel, grid=(NB, S // SWA_W),
        in_specs=[pl.BlockSpec(memory_space=pltpu.SMEM), qspec,
                  band(0, 0), band(1, 0), band(2, 0), ctx(0), band(0, 1), band(1, 1), band(2, 1), ctx(1),
                  pl.BlockSpec((1,) + masks.shape[1:], variant)],
        out_specs=qspec,
        out_shape=jax.ShapeDtypeStruct((NB, S, SWA_H * HD), BF16),
        compiler_params=_params(2), name="swa")(sink, q, kv, kv, kv, kv, kv, kv, kv, kv, masks)


N_DR = 2 * NA_KH - 1
N_DC = 2 * NA_KW - 1
NA_G = 4
NA_WIN = NA_G + NA_KH
NA_QB = NA_G * GW
NA_KB = NA_WIN * GW // NA_QB


def _na_bias_kernel(rpb_ref, o_ref):
    h = pl.program_id(0)
    shape = (GW, 2 * GW)
    lane = lax.broadcasted_iota(I32, shape, 1)
    qc = lax.broadcasted_iota(I32, shape, 0)
    kc = lane & (GW - 1)
    diff = kc - qc + (NA_KW - 1)
    cs = jnp.clip(qc - NA_KW // 2, 0, GW - NA_KW)
    ok = (kc >= cs) & (kc < cs + NA_KW)
    right = lax.broadcasted_iota(I32, (1, 2 * GW), 1) >= GW
    for dd in range(N_DR + 1):
        acc = jnp.zeros(shape, F32)
        for m in range(N_DC):
            lo = rpb_ref[(h * N_DR + dd - 1) * N_DC + m] if dd >= 1 else 0.0
            hi = rpb_ref[(h * N_DR + dd) * N_DC + m] if dd < N_DR else 0.0
            acc = jnp.where(diff == m, jnp.where(right, hi, lo), acc)
        valid = ok & (right if dd == 0 else (~right if dd == N_DR else True))
        o_ref[0, dd] = jnp.where(valid, acc * LOG2E, NEG)


def _na_bias(rpb):
    return pl.pallas_call(
        _na_bias_kernel, grid=(NA_H,),
        in_specs=[pl.BlockSpec(memory_space=pltpu.SMEM)],
        out_specs=pl.BlockSpec((1, N_DR + 1, GW, 2 * GW), lambda h: (h, 0, 0, 0)),
        out_shape=jax.ShapeDtypeStruct((NA_H, N_DR + 1, GW, 2 * GW), F32),
        compiler_params=_params(1), name="na_bias")(rpb.reshape(-1))


def _na_kernel(*refs):
    q_ref = refs[0]
    k_refs = refs[1:1 + NA_KB]
    v_refs = refs[1 + NA_KB:1 + 2 * NA_KB]
    kc_ref, vc_ref, tz_ref, o_ref = refs[1 + 2 * NA_KB:]
    scale = HD ** -0.5 * LOG2E
    nt = (((1,), (1,)), ((), ()))
    rows = L // GW
    g = pl.program_id(1)
    is_lat = g < rows // NA_G
    r0 = g * NA_G
    ustart = jnp.clip(r0 - NA_KH // 2, 0, rows - NA_WIN)
    lane = lax.broadcasted_iota(I32, (1, NA_WIN * GW), 1)
    masks, dds = [], []
    for a in range(NA_G):
        r = r0 + a
        lo = (jnp.clip(r - NA_KH // 2, 0, rows - NA_KH) - ustart) * GW
        hi = jnp.where(is_lat, lo + NA_KH * GW, lo)
        masks.append((lane >= lo) & (lane < hi))
        dds.append([jnp.clip(ustart + 2 * p - r + NA_KH, 0, N_DR) for p in range(NA_WIN // 2)])
    for h in range(NA_H):
        hs = slice(h * HD, (h + 1) * HD)
        qh = q_ref[0, :, hs]
        kh = jnp.concatenate([r[0, :, hs] for r in k_refs], axis=0)
        vh = jnp.concatenate([r[0, :, hs] for r in v_refs], axis=0)
        s = lax.dot_general(qh, kh, nt, preferred_element_type=F32) * scale
        parts = []
        for a in range(NA_G):
            bias = jnp.concatenate([tz_ref[h, dd] for dd in dds[a]], axis=1)
            parts.append(jnp.where(masks[a], s[a * GW:(a + 1) * GW] + bias, NEG))
        s = jnp.concatenate(parts, axis=0)
        sc = lax.dot_general(qh, kc_ref[0, :, hs], nt, preferred_element_type=F32) * scale
        m = jnp.maximum(jnp.max(s, axis=1, keepdims=True), jnp.max(sc, axis=1, keepdims=True))
        p = jnp.exp2(s - m)
        pc = jnp.exp2(sc - m)
        den = jnp.sum(p, axis=1, keepdims=True) + jnp.sum(pc, axis=1, keepdims=True)
        o = jnp.dot(p.astype(BF16), vh, preferred_element_type=F32)
        o = o + jnp.dot(pc.astype(BF16), vc_ref[0, :, hs], preferred_element_type=F32)
        o_ref[0, :, hs] = (o / den).astype(o_ref.dtype)


def _na(qkv, tz):
    rows = L // GW
    assert LC == NA_QB and (rows - NA_WIN) % NA_G == 0

    def win(j, part):
        return pl.BlockSpec(
            (1, NA_QB, D), lambda b, g: (b, jnp.clip(g * NA_G - NA_KH // 2, 0, rows - NA_WIN) // NA_G + j, part))

    in_specs = [pl.BlockSpec((1, NA_QB, D), lambda b, g: (b, g, 0))]
    in_specs += [win(j, 1) for j in range(NA_KB)] + [win(j, 2) for j in range(NA_KB)]
    in_specs += [pl.BlockSpec((1, LC, D), lambda b, g: (b, L // LC, 1)),
                 pl.BlockSpec((1, LC, D), lambda b, g: (b, L // LC, 2)),
                 pl.BlockSpec(tz.shape, lambda b, g: (0, 0, 0, 0))]
    return pl.pallas_call(
        _na_kernel, grid=(NB, S // NA_QB),
        in_specs=in_specs,
        out_specs=pl.BlockSpec((1, NA_QB, D), lambda b, g: (b, g, 0)),
        out_shape=jax.ShapeDtypeStruct((NB, S, D), BF16),
        compiler_params=_params(2), name="na")(*([qkv] * (3 + 2 * NA_KB)), tz)


def _filter_kernel(z_ref, w1_ref, b1_ref, w2_ref, b2_ref, w3_ref, fr_ref, dl_ref, o_ref):
    hp = lax.Precision.HIGHEST
    z = z_ref[...]
    fr = fr_ref[...]
    h = jnp.sin(fr * (jnp.dot(z, w1_ref[...], precision=hp, preferred_element_type=F32) + b1_ref[...]))
    for i in range(w2_ref.shape[0]):
        h = jnp.sin(fr * (jnp.dot(h, w2_ref[i], precision=hp, preferred_element_type=F32) + b2_ref[i]))
    h = jnp.dot(h.astype(BF16), w3_ref[...].astype(BF16), preferred_element_type=F32)
    decay = jnp.exp(-z[:, 0:1] * dl_ref[...])
    parts = []
    for order in range(2):
        fwd = h[:, (2 * order) * HY:(2 * order + 1) * HY] * decay
        bwd = h[:, (2 * order + 1) * HY:(2 * order + 2) * HY] * decay
        parts += [fwd + bwd, fwd - bwd]
    o_ref[...] = jnp.concatenate(parts, axis=1).astype(o_ref.dtype)


def _filter_embedding(n):
    t = jnp.linspace(0.0, 1.0, n, dtype=F32)[:, None]
    w = (2.0 * math.pi / n) * jnp.arange(n, dtype=F32)[:, None]
    f = jnp.linspace(1e-4, HY_BANDS - 1, HY_BANDS, dtype=F32)[None, :]
    z = jnp.concatenate([t, jnp.cos(f * w), -jnp.sin(f * w)], axis=-1)
    return jnp.pad(z, ((0, 0), (0, HD - HY_EMB)))


def _hyena_filters(n, fw1, fb1, fw2, fb2, fw3, ffreq):
    deltas = jnp.abs(jnp.linspace(math.log(HY_TARGET) / HY_SLOW_DECAY, math.log(HY_TARGET) / HY_FAST_DECAY,
                                  HY, dtype=F32))[None, :]
    w1 = jnp.pad(fw1, ((0, HD - HY_EMB), (0, 0)))
    tm = min(n, 256)
    nout = fw3.shape[1]
    full = lambda a: pl.BlockSpec(a.shape, lambda i: (0,) * a.ndim)
    args = (_filter_embedding(n), w1, fb1.reshape(1, -1), fw2, fb2.reshape(fb2.shape[0], 1, -1), fw3,
            ffreq.reshape(1, -1), deltas)
    return pl.pallas_call(
        _filter_kernel, grid=(n // tm,),
        in_specs=[pl.BlockSpec((tm, HD), lambda i: (i, 0))] + [full(a) for a in args[1:]],
        out_specs=pl.BlockSpec((tm, nout), lambda i: (i, 0)),
        out_shape=jax.ShapeDtypeStruct((n, nout), BF16),
        compiler_params=_params(1), name="hy_filter")(*args)


def _shortconv_kernel(u_ref, w_ref, b_ref, o_ref):
    u = u_ref[0].astype(F32)
    t = lax.broadcasted_iota(I32, (S, 1), 0)
    prev = jnp.where((t == 0) | (t == L), 0.0, pltpu.roll(u, 1, 0))
    nxt = jnp.where((t == L - 1) | (t == S - 1), 0.0, pltpu.roll(u, S - 1, 0))
    w = w_ref[...]
    o_ref[0] = (b_ref[...] + prev * w[0:1] + u * w[1:2] + nxt * w[2:3]).astype(o_ref.dtype)


def _shortconv(u, w, b):
    tc = 256
    nc = u.shape[2]
    return pl.pallas_call(
        _shortconv_kernel, grid=(NB, nc // tc),
        in_specs=[pl.BlockSpec((1, S, tc), lambda b_, c: (b_, 0, c)),
                  pl.BlockSpec((3, tc), lambda b_, c: (0, c)),
                  pl.BlockSpec((1, tc), lambda b_, c: (0, c))],
        out_specs=pl.BlockSpec((1, S, tc), lambda b_, c: (b_, 0, c)),
        out_shape=jax.ShapeDtypeStruct(u.shape, BF16),
        compiler_params=_params(2), name="shortconv")(u, w, b.reshape(1, -1))


def _dft_tables(n):
    nn = 2 * n
    lo = 64
    f = jnp.arange(n, dtype=I32)[:, None]
    angle = lambda m: (m % nn).astype(F32) * (2.0 * math.pi / nn)
    a = angle(f * (jnp.arange(n // lo, dtype=I32) * lo)[None, :])[:, :, None]
    b = angle(f * jnp.arange(lo, dtype=I32)[None, :])[:, None, :]
    c = (jnp.cos(a) * jnp.cos(b) - jnp.sin(a) * jnp.sin(b)).reshape(n, n)
    s = -(jnp.sin(a) * jnp.cos(b) + jnp.cos(a) * jnp.sin(b)).reshape(n, n)
    i = jnp.arange(n, dtype=I32)
    alt = jnp.where(i % 2 == 0, 1.0, -1.0).astype(F32)
    fwd = jnp.concatenate([c, jnp.where(i[:, None] == 0, alt[None, :], s)], axis=0)
    inv_r = c * jnp.where(i == 0, 1.0 / nn, 2.0 / nn).astype(F32)[None, :]
    inv_i = jnp.where(i[None, :] == 0, alt[:, None] / nn, s * (2.0 / nn))
    return fwd.astype(BF16), inv_r.astype(BF16), inv_i.astype(BF16)


HY_TN = 512
HY_NCT = HY // HY_TN


def _filt_dft_kernel(c_ref, s_ref, hs_ref, hd_ref, kr_ref, ki_ref):
    kr_ref[0] = jnp.dot(c_ref[...], hs_ref[...], preferred_element_type=F32)
    ki_ref[0] = jnp.dot(s_ref[...], hd_ref[...], preferred_element_type=F32)

    @pl.when(pl.program_id(0) == 0)
    def _():
        nyq = jnp.dot(s_ref[0:8, :], hs_ref[...], preferred_element_type=F32)
        first = lax.broadcasted_iota(I32, nyq.shape, 0) == 0
        ki_ref[0, 0:8, :] = jnp.where(first, nyq, ki_ref[0, 0:8, :])


def _filt_dft(fwd, hfilt, n):
    tf = min(n, 1024)
    nf = n // tf
    out = pl.BlockSpec((1, tf, HY_TN), lambda f, c: (c // HY_NCT, f, c % HY_NCT))
    return pl.pallas_call(
        _filt_dft_kernel, grid=(nf, 2 * HY_NCT),
        in_specs=[pl.BlockSpec((tf, n), lambda f, c: (f, 0)),
                  pl.BlockSpec((tf, n), lambda f, c: (nf + f, 0)),
                  pl.BlockSpec((n, HY_TN), lambda f, c: (0, (c // HY_NCT) * 2 * HY_NCT + c % HY_NCT)),
                  pl.BlockSpec((n, HY_TN), lambda f, c: (0, (c // HY_NCT) * 2 * HY_NCT + HY_NCT + c % HY_NCT))],
        out_specs=[out, out],
        out_shape=[jax.ShapeDtypeStruct((2, n, HY), F32)] * 2,
        compiler_params=_params(2), name=f"filt_dft{n}")(fwd, fwd, hfilt, hfilt)


def _dft_mul_kernel(z_ref, c_ref, s_ref, kr_ref, ki_ref, yr_ref, yi_ref):
    z = z_ref[0].astype(BF16)
    zr = jnp.dot(c_ref[...], z, preferred_element_type=F32)
    zi = jnp.dot(s_ref[...], z, preferred_element_type=F32)
    kr = kr_ref[...]
    ki = ki_ref[...]
    first = (lax.broadcasted_iota(I32, zr.shape, 0) == 0) & (pl.program_id(0) == 0)
    ii = zi * ki
    yr_ref[0] = (zr * kr - jnp.where(first, 0.0, ii)).astype(yr_ref.dtype)
    yi_ref[0] = jnp.where(first, ii, zr * ki + zi * kr).astype(yi_ref.dtype)


def _dft_mul(z, zrow, zc0, n, fwd, kr, ki, order):
    tf = min(n, 1024)
    nf = n // tf
    out = pl.BlockSpec((1, tf, HY_TN), lambda f, b, c: (b, f, c))
    kspec = lambda: pl.BlockSpec((None, tf, HY_TN), lambda f, b, c: (order, f, c))
    return pl.pallas_call(
        _dft_mul_kernel, grid=(nf, NB, HY_NCT),
        in_specs=[pl.BlockSpec((1, n, HY_TN), lambda f, b, c: (b, zrow, zc0 + c)),
                  pl.BlockSpec((tf, n), lambda f, b, c: (f, 0)),
                  pl.BlockSpec((tf, n), lambda f, b, c: (nf + f, 0)),
                  kspec(), kspec()],
        out_specs=[out, out],
        out_shape=[jax.ShapeDtypeStruct((NB, n, HY), BF16)] * 2,
        compiler_params=_params(3), name=f"dft_mul{n}")(z, fwd, fwd, kr, ki)


def _idft_gate_kernel(ar_ref, ai_ref, yr_ref, yi_ref, z_ref, p_ref, bias_ref, *o_refs):
    y = jnp.dot(ar_ref[...], yr_ref[0], preferred_element_type=F32)
    y = y + jnp.dot(ai_ref[...], yi_ref[0], preferred_element_type=F32)
    out = p_ref[0].astype(F32) * (y + z_ref[0].astype(F32) * bias_ref[0])
    for o_ref in o_refs:
        o_ref[0] = out.astype(o_ref.dtype)


def _idft_gate(yr, yi, inv_r, inv_i, n, z, z_r0, z_c0, p, p_r0, p_c0, bias, order, out_dtype):
    tm = min(n, 1024)
    return pl.pallas_call(
        _idft_gate_kernel, grid=(n // tm, NB, HY_NCT),
        in_specs=[pl.BlockSpec((tm, n), lambda t, b, c: (t, 0)),
                  pl.BlockSpec((tm, n), lambda t, b, c: (t, 0)),
                  pl.BlockSpec((1, n, HY_TN), lambda t, b, c: (b, 0, c)),
                  pl.BlockSpec((1, n, HY_TN), lambda t, b, c: (b, 0, c)),
                  pl.BlockSpec((1, tm, HY_TN), lambda t, b, c: (b, z_r0 + t, z_c0 + c)),
                  pl.BlockSpec((1, tm, HY_TN), lambda t, b, c: (b, p_r0 + t, p_c0 + c)),
                  pl.BlockSpec((1, 1, HY_TN), lambda t, b, c: (order, 0, c))],
        out_specs=pl.BlockSpec((1, tm, HY_TN), lambda t, b, c: (b, t, c)),
        out_shape=jax.ShapeDtypeStruct((NB, n, HY), out_dtype),
        compiler_params=_params(3), name=f"idft_gate{n}_{order}")(inv_r, inv_i, yr, yi, z, p, bias)


def _hyena(u, conv_w, conv_b, filt, hy_bias, tables):
    sc = _shortconv(u, conv_w, conv_b)
    bias = hy_bias.reshape(2, 1, HY)
    outs = []
    for n in (L, LC):
        zrow = 0 if n == L else L // LC
        r0 = zrow * (n // min(n, 1024))
        fwd, inv_r, inv_i = tables[n]
        kr, ki = _filt_dft(fwd, _hyena_filters(n, *filt), n)
        yr, yi = _dft_mul(sc, zrow, 2 * HY_NCT, n, fwd, kr, ki, 0)
        z1 = _idft_gate(yr, yi, inv_r, inv_i, n, sc, r0, 2 * HY_NCT, sc, r0, 0, bias, 0, BF16)
        yr, yi = _dft_mul(z1, 0, 0, n, fwd, kr, ki, 1)
        outs.append(_idft_gate(yr, yi, inv_r, inv_i, n, z1, 0, 0, sc, r0, HY_NCT, bias, 1, BF16))
    return outs


def _resid_ln(x, y, gate, g, b):
    v = DN_ALPHA * x + gate * y
    mu = jnp.mean(v, axis=-1, keepdims=True)
    vc = v - mu
    var = jnp.mean(vc * vc, axis=-1, keepdims=True)
    return vc * lax.rsqrt(var + LN_EPS) * g + b


def _wout_ln_kernel(a1a_ref, a1b_ref, a1c_ref, a1x_ref, a2_ref, w_ref, x_ref, gl_ref, gc_ref, g_ref, b_ref,
                    scl_ref, scc_ref, shl_ref, shc_ref, wrh_ref, wrl_ref, xo_ref, hm_ref, aff_ref):
    half = a2_ref.shape[2]
    last_tile = pl.program_id(1) == pl.num_programs(1) - 1
    a1_refs = (a1a_ref, a1b_ref, a1c_ref)
    n_sub = len(a1_refs)

    def project(s):
        a1 = jnp.where(last_tile, a1x_ref[0], a1_refs[s][0]) if s == n_sub - 1 else a1_refs[s][0]
        y = jnp.dot(a1, w_ref[0:half, :], preferred_element_type=F32)
        return y + jnp.dot(a2_ref[0, s * TM:(s + 1) * TM, :], w_ref[half:2 * half, :], preferred_element_type=F32)

    for s in range(n_sub):
        rows = slice(s * TM, (s + 1) * TM)
        if s == n_sub - 1:
            pick = lambda lat_ref, ctx_ref: jnp.where(last_tile, ctx_ref[0], lat_ref[0])
        else:
            pick = lambda lat_ref, ctx_ref: lat_ref[0]
        xn = _resid_ln(x_ref[0, rows, :], project(s), pick(gl_ref, gc_ref), g_ref[0], b_ref[0])
        xo_ref[0, rows, :] = xn
        hm = xn * (1.0 + pick(scl_ref, scc_ref)) + pick(shl_ref, shc_ref)
        hm_hi = hm.astype(BF16)
        hm_ref[0, rows, :] = hm_hi
        hm_lo = (hm - hm_hi.astype(F32)).astype(BF16)
        logits = jnp.dot(hm_hi, wrh_ref[...], preferred_element_type=F32)
        logits = logits + jnp.dot(hm_lo, wrh_ref[...], preferred_element_type=F32)
        logits = logits + jnp.dot(hm_hi, wrl_ref[...], preferred_element_type=F32)
        e = jnp.exp(logits - jnp.max(logits, axis=-1, keepdims=True))
        aff_ref[0, rows, :] = e / jnp.sum(e, axis=-1, keepdims=True)


def _wout_ln(a1_lat, a1_ctx, a1_ctx_blk, c1, a2, c2, w_bf, x, mod, layer, lng, lnb, w_router):
    half = D // 2
    sub = TMW // TM
    n_lat_blk = a1_lat.shape[1] // TM
    tile = lambda: pl.BlockSpec((1, TMW, D), lambda b, i: (b, i, 0))
    vec = lambda idx: pl.BlockSpec((1, 1, D), idx)
    lat = lambda k: vec(lambda b, i: ((layer * 8 + b) * 6 + k, 0, 0))
    ctx = lambda k: vec(lambda b, i: ((layer * 8 + NB) * 6 + k, 0, 0))
    a1_blk = lambda s: pl.BlockSpec((1, TM, half), lambda b, i: (b, jnp.minimum(i * sub + s, n_lat_blk - 1), c1))
    wr_hi = w_router.astype(BF16)
    wr_lo = (w_router - wr_hi.astype(F32)).astype(BF16)
    return pl.pallas_call(
        _wout_ln_kernel, grid=(NB, S // TMW),
        in_specs=[a1_blk(0), a1_blk(1), a1_blk(2),
                  pl.BlockSpec((1, TM, half), lambda b, i: (b, a1_ctx_blk, c1)),
                  pl.BlockSpec((1, TMW, half), lambda b, i: (b, i, c2)),
                  pl.BlockSpec((D, D), lambda b, i: (0, 0), pipeline_mode=pl.Buffered(1)),
                  tile(), lat(2), ctx(2),
                  vec(lambda b, i: (2 * layer, 0, 0)), vec(lambda b, i: (2 * layer, 0, 0)),
                  lat(4), ctx(4), lat(3), ctx(3),
                  pl.BlockSpec((D, NE), lambda b, i: (0, 0)),
                  pl.BlockSpec((D, NE), lambda b, i: (0, 0))],
        out_specs=[tile(), tile(), pl.BlockSpec((1, TMW, NE), lambda b, i: (b, i, 0))],
        out_shape=[jax.ShapeDtypeStruct((NB, S, D), F32), jax.ShapeDtypeStruct((NB, S, D), BF16),
                   jax.ShapeDtypeStruct((NB, S, NE), F32)],
        compiler_params=_params(2), name="wout_ln")(
            a1_lat, a1_lat, a1_lat, a1_ctx, a2, w_bf, x, mod, mod, lng, lnb, mod, mod, mod, mod, wr_hi, wr_lo)


def _route_kernel(a_ref, o_ref, tri_ref):
    rb = 256
    for r in range(L // rb):
        s_i = lax.broadcasted_iota(I32, (rb, L), 0) + r * rb
        t_i = lax.broadcasted_iota(I32, (rb, L), 1)
        tri_ref[r * rb:(r + 1) * rb, :] = jnp.where(s_i < t_i, 1.0, 0.0).astype(BF16)

    def select(a, cap, base):
        n = a.shape[1]

        def body(i, prefix):
            cand = prefix | jnp.left_shift(jnp.int32(1), 30 - i)
            cnt = jnp.sum(jnp.where(a >= lax.bitcast_convert_type(cand, F32), 1.0, 0.0), axis=1, keepdims=True)
            return jnp.where(cnt >= cap, cand, prefix)

        kth = lax.bitcast_convert_type(lax.fori_loop(0, 31, body, jnp.zeros((a.shape[0], 1), I32)), F32)
        gt = a > kth
        eq = a == kth
        n_gt = jnp.sum(jnp.where(gt, 1.0, 0.0), axis=1, keepdims=True)
        tri = tri_ref[0:n, 0:n]
        before = jnp.dot(jnp.where(eq, 1.0, 0.0).astype(BF16), tri, preferred_element_type=F32)
        sel = gt | (eq & (before < cap - n_gt))
        pos = jnp.dot(jnp.where(sel, 1.0, 0.0).astype(BF16), tri, preferred_element_type=F32)
        return jnp.where(sel, pos.astype(I32) + base, -1)

    a = a_ref[...]
    o_ref[...] = jnp.concatenate([select(a[:, :L], CAP_L, 0), select(a[:, L:], CAP_C, CAP_L)], axis=1)


def _route(aff_t):
    return pl.pallas_call(
        _route_kernel, grid=(1,),
        in_specs=[pl.BlockSpec((NB * NE, S), lambda i: (0, 0))],
        out_specs=pl.BlockSpec((NB * NE, S), lambda i: (0, 0)),
        out_shape=jax.ShapeDtypeStruct((NB * NE, S), I32),
        scratch_shapes=[pltpu.VMEM((L, L), BF16)],
        compiler_params=_params(1), name="route")(aff_t.reshape(NB * NE, S)).reshape(NB, NE, S)


GATHER_EG = 4


def _gather_kernel(slot_ref, aff_ref, h_ref, xs_ref, g_ref):
    for r0, cap, c0, n in ((0, CAP_L, 0, L), (CAP_L, CAP_C, L, LC)):
        want = lax.broadcasted_iota(I32, (cap, n), 0) + r0
        hits = []
        for e in range(GATHER_EG):
            hit = slot_ref[0, e][:, c0:c0 + n] == want
            hits.append(jnp.where(hit, 1.0, 0.0).astype(BF16))
            g = jnp.sum(jnp.where(hit, aff_ref[0, e][:, c0:c0 + n], 0.0), axis=1, keepdims=True)
            g_ref[0, e, r0:r0 + cap, :] = jnp.broadcast_to(g, (cap, HD))
        xs = jnp.dot(jnp.concatenate(hits, axis=0), h_ref[0, c0:c0 + n, :], preferred_element_type=F32)
        for e in range(GATHER_EG):
            xs_ref[0, e, r0:r0 + cap, :] = xs[e * cap:(e + 1) * cap].astype(xs_ref.dtype)


def _gather(slot, aff_t, hm):
    row = lambda: pl.BlockSpec((1, GATHER_EG, 1, S), lambda b, e: (b, e, 0, 0))
    return pl.pallas_call(
        _gather_kernel, grid=(NB, NE // GATHER_EG),
        in_specs=[row(), row(), pl.BlockSpec((1, S, D), lambda b, e: (b, 0, 0))],
        out_specs=[pl.BlockSpec((1, GATHER_EG, CAP, D), lambda b, e: (b, e, 0, 0)),
                   pl.BlockSpec((1, GATHER_EG, CAP, HD), lambda b, e: (b, e, 0, 0))],
        out_shape=[jax.ShapeDtypeStruct((NB, NE, CAP, D), BF16), jax.ShapeDtypeStruct((NB, NE, CAP, HD), F32)],
        compiler_params=_params(2), name="moe_gather")(
            slot.reshape(NB, NE, 1, S), aff_t.reshape(NB, NE, 1, S), hm)


FFN_SPLIT = 4


def _ffn_kernel(xs_ref, g_ref, w1_ref, w3_ref, w2_ref, ol_ref, oc_ref, hmid_ref):
    f = pl.program_id(1)
    tf = FF // FFN_SPLIT

    @pl.when(f < FFN_SPLIT)
    def _():
        x = xs_ref[:, 0].reshape(NB * CAP, D)
        a = jnp.dot(x, w1_ref[0].astype(BF16), preferred_element_type=F32)
        u = jnp.dot(x, w3_ref[0].astype(BF16), preferred_element_type=F32)
        hmid_ref[:, pl.ds(pl.multiple_of(f * tf, tf), tf)] = (a * jax.nn.sigmoid(a) * u).astype(BF16)

    @pl.when(f >= FFN_SPLIT)
    def _():
        y = jnp.dot(hmid_ref[...], w2_ref[0].astype(BF16), preferred_element_type=F32)
        g = g_ref[:, 0].reshape(NB * CAP, HD)[:, 0:1]
        y = (y * g).astype(ol_ref.dtype).reshape(NB, CAP, y.shape[1])
        ol_ref[:, 0] = y[:, :CAP_L]
        oc_ref[:, 0] = y[:, CAP_L:]


def _ffn(xs, gate, w1, w3, w2, layer):
    tf = FF // FFN_SPLIT
    tn = D // FFN_SPLIT
    blk = lambda width: pl.BlockSpec((NB, 1, CAP, width), lambda e, f: (0, e, 0, 0))
    hid = lambda e, f: (layer, e, 0, jnp.minimum(f, FFN_SPLIT - 1))
    col = lambda f: jnp.maximum(f - FFN_SPLIT, 0)
    out = lambda cap: pl.BlockSpec((NB, 1, cap, tn), lambda e, f: (0, e, 0, col(f)))
    return pl.pallas_call(
        _ffn_kernel, grid=(NE, 2 * FFN_SPLIT),
        in_specs=[blk(D), blk(HD),
                  pl.BlockSpec((None, 1, D, tf), hid),
                  pl.BlockSpec((None, 1, D, tf), hid),
                  pl.BlockSpec((None, 1, FF, tn), lambda e, f: (layer, e, 0, col(f)))],
        out_specs=[out(CAP_L), out(CAP_C)],
        out_shape=[jax.ShapeDtypeStruct((NB, NE, CAP_L, D), BF16), jax.ShapeDtypeStruct((NB, NE, CAP_C, D), BF16)],
        scratch_shapes=[pltpu.VMEM((NB * CAP, FF), BF16)],
        compiler_params=_params(2), name="moe_ffn")(xs, gate, w1, w3, w2)


def _combine_ln_kernel(*refs, last):
    if last:
        slot_ref, yl_ref, x_ref, gate_ref, g_ref, b_ref, xo_ref, acc_ref = refs
    else:
        slot_ref, yl_ref, yc_ref, x_ref, gate_ref, g_ref, b_ref, sc_ref, sh_ref, xo_ref, hn_ref, acc_ref = refs
    st = slot_ref[0]

    def scatter(y_ref, r0, cap):
        want = lax.broadcasted_iota(I32, (TM, cap), 1) + r0
        hits = [jnp.where(st[:, e:e + 1] == want, 1.0, 0.0).astype(BF16) for e in range(NE)]
        if cap % HD == 0:
            acc_ref[...] = jnp.dot(jnp.concatenate(hits, axis=1), y_ref[0].reshape(NE * cap, D),
                                   preferred_element_type=F32)
        else:
            acc = None
            for e in range(NE):
                part = jnp.dot(hits[e], y_ref[0, e], preferred_element_type=F32)
                acc = part if acc is None else acc + part
            acc_ref[...] = acc

    if last:
        scatter(yl_ref, 0, CAP_L)
    else:
        is_lat = pl.program_id(1) < N_LAT_TILES
        pl.when(is_lat)(lambda: scatter(yl_ref, 0, CAP_L))
        pl.when(jnp.logical_not(is_lat))(lambda: scatter(yc_ref, CAP_L, CAP_C))
    xn = _resid_ln(x_ref[0], acc_ref[...], gate_ref[0], g_ref[0], b_ref[0])
    xo_ref[0] = xn
    if not last:
        hn_ref[0] = (xn * (1.0 + sc_ref[0]) + sh_ref[0]).astype(hn_ref.dtype)


def _combine_ln(slot_t, yl, yc, x, mod, layer, lng, lnb, last):
    rows = L if last else S
    tile = lambda: pl.BlockSpec((1, TM, D), lambda b, i: (b, i, 0))
    ln_vec = lambda: pl.BlockSpec((1, 1, D), lambda b, i: (2 * layer + 1, 0, 0))
    in_specs = [pl.BlockSpec((1, TM, NE), lambda b, i: (b, i, 0)),
                pl.BlockSpec((1, NE, CAP_L, D), lambda b, i: (b, 0, 0, 0), pipeline_mode=pl.Buffered(1))]
    args = [slot_t, yl]
    if not last:
        in_specs.append(pl.BlockSpec((1, NE, CAP_C, D), lambda b, i: (b, 0, 0, 0)))
        args.append(yc)
    in_specs += [tile(), _mod_spec(layer, 5), ln_vec(), ln_vec()]
    args += [x, mod, lng, lnb]
    out_specs = [tile()]
    out_shape = [jax.ShapeDtypeStruct((NB, rows, D), F32)]
    if not last:
        in_specs += [_mod_spec(layer + 1, 1), _mod_spec(layer + 1, 0)]
        args += [mod, mod]
        out_specs.append(tile())
        out_shape.append(jax.ShapeDtypeStruct((NB, rows, D), BF16))
    return pl.pallas_call(
        functools.partial(_combine_ln_kernel, last=last), grid=(NB, rows // TM),
        in_specs=in_specs, out_specs=out_specs, out_shape=out_shape,
        scratch_shapes=[pltpu.VMEM((TM, D), F32)],
        compiler_params=_params(2), name="moe_combine_ln")(*args)


def _moe(hm, aff, x, mod, layer, lng, lnb, w1, w3, w2, last):
    aff_t = jnp.swapaxes(aff, 1, 2)
    slot = _route(aff_t)
    xs, gate = _gather(slot, aff_t, hm)
    yl, yc = _ffn(xs, gate, w1, w3, w2, layer)
    return _combine_ln(jnp.swapaxes(slot, 1, 2), yl, yc, x, mod, layer, lng, lnb, last)


def kernel(x, c, ctx, c_ctx, ada_w, ada_b, ln_g, ln_b, ev_w_in, ev_w_out, hy_conv_w, hy_conv_b, hy_f_w1, hy_f_b1,
           hy_f_w2, hy_f_b2, hy_f_w3, hy_f_freq, hy_bias, swa_sink, od_w_in, od_w_out, na_rpb, moe_w_router,
           moe_w1, moe_w3, moe_w2):
    assert x.shape == (NB, L, D) and ctx.shape == (NB, LC, D)
    xa = jnp.concatenate([x, ctx], axis=1)
    cc = jnp.concatenate([c, c_ctx[None, :], jnp.zeros((8 - NB - 1, D), F32)], axis=0)
    mod = _ada(cc, ada_w, ada_b).reshape(DEPTH * 8 * 6, 1, D)
    lng = ln_g.reshape(DEPTH * 2, 1, D)
    lnb = ln_b.reshape(DEPTH * 2, 1, D)
    rope = _rope_tables()
    tables = {L: _dft_tables(L), LC: _dft_tables(LC)}
    swa_masks = _swa_masks()
    h = _modulate(xa, mod, 0)
    for l in range(DEPTH):
        i = l // 2
        last = l == DEPTH - 1
        if l % 2 == 0:
            u = _proj(h, ev_w_in, i, 0, EV_Q0, 1024, 0, rope, BF16, "proj_hy")
            q = _proj(h, ev_w_in, i, EV_Q0, SWA_H * HD, 1024, SWA_H * HD, rope, BF16, "proj_q")
            kv = _proj(h, ev_w_in, i, EV_K0, 2 * SWA_KV * HD, 512, SWA_KV * HD, rope, BF16, "proj_kv")
            att = _swa(q, kv, swa_sink[i], swa_masks)
            filt = (hy_f_w1[i], hy_f_b1[i], hy_f_w2[i], hy_f_b2[i], hy_f_w3[i], hy_f_freq[i])
            z_lat, z_ctx = _hyena(u, hy_conv_w[i], hy_conv_b[i], filt, hy_bias[i], tables)
            xa, hm, aff = _wout_ln(z_lat, z_ctx, 0, 0, att, 0, ev_w_out[i].astype(BF16), xa, mod, l, lng, lnb,
                                   moe_w_router[l])
        else:
            qkv = _proj(h, od_w_in, i, 0, 3 * D, 1024, 0, rope, BF16, "proj_qkv")
            att = _na(qkv, _na_bias(na_rpb[i]))
            xa, hm, aff = _wout_ln(att, att, L // TM, 0, att, 1, od_w_out[i].astype(BF16), xa, mod, l, lng, lnb,
                                   moe_w_router[l])
        out = _moe(hm, aff, xa, mod, l, lng, lnb, moe_w1, moe_w3, moe_w2, last)
        if last:
            return out[0]
        xa, h = out
```

```python
import functools
import math

import jax
import jax.numpy as jnp
from jax import lax
from jax.experimental import pallas as pl
from jax.experimental.pallas import tpu as pltpu

F32 = jnp.float32
BF16 = jnp.bfloat16
I32 = jnp.int32

D = 2048
NB = 4
L = 2048
LC = 256
S = L + LC
DEPTH = 4
GW = 64
HD = 128
HY = D // 2
HY_BANDS = 16
HY_EMB = 2 * HY_BANDS + 1
HY_HID = 64
HY_FAST_DECAY = 0.3
HY_SLOW_DECAY = 1.5
HY_TARGET = 1e-2
SWA_H = (D - HY) // HD
SWA_KV = SWA_H // 4
SWA_G = SWA_H // SWA_KV
SWA_W = 128
NA_H = D // HD
NA_KH = 8
NA_KW = 16
ROPE_BASE = 10000.0
NE = 16
CAP_L = 2 * L // NE
CAP_C = 2 * LC // NE
CAP = CAP_L + CAP_C
FF = D // 2
DN_ALPHA = (2 * DEPTH) ** 0.25
LN_EPS = 1e-5
EV_Q0 = 3 * HY
EV_K0 = EV_Q0 + SWA_H * HD
EV_IN = EV_K0 + 2 * SWA_KV * HD
NEG = -1e30
LOG2E = math.log2(math.e)

TM = 256
N_LAT_TILES = L // TM
TMW = 3 * TM
VMEM_LIMIT = 56 * 1024 * 1024


def _params(n_axes, vmem=VMEM_LIMIT):
    return pltpu.CompilerParams(dimension_semantics=("arbitrary",) * n_axes, vmem_limit_bytes=vmem)


def _mod_spec(layer, k):
    def index(b, i):
        row = jnp.where(i >= N_LAT_TILES, NB, b)
        return ((layer * 8 + row) * 6 + k, 0, 0)
    return pl.BlockSpec((1, 1, D), index)


def _ada_kernel(c_ref, w_ref, b_ref, o_ref):
    c = c_ref[...]
    s = c * jax.nn.sigmoid(c)
    o_ref[0] = jnp.dot(s.astype(BF16), w_ref[0].astype(BF16), preferred_element_type=F32) + b_ref[0]


def _ada(cc, ada_w, ada_b):
    tn = 1024
    return pl.pallas_call(
        _ada_kernel, grid=(DEPTH, 6 * D // tn),
        in_specs=[pl.BlockSpec((8, D), lambda l, j: (0, 0)),
                  pl.BlockSpec((1, D, tn), lambda l, j: (l, 0, j)),
                  pl.BlockSpec((1, 1, tn), lambda l, j: (l, 0, j))],
        out_specs=pl.BlockSpec((1, 8, tn), lambda l, j: (l, 0, j)),
        out_shape=jax.ShapeDtypeStruct((DEPTH, 8, 6 * D), F32),
        compiler_params=_params(2), name="ada")(cc, ada_w, ada_b.reshape(DEPTH, 1, 6 * D))


def _mod_kernel(x_ref, sc_ref, sh_ref, o_ref):
    o_ref[0] = (x_ref[0] * (1.0 + sc_ref[0]) + sh_ref[0]).astype(o_ref.dtype)


def _modulate(x, mod, layer):
    return pl.pallas_call(
        _mod_kernel, grid=(NB, S // TM),
        in_specs=[pl.BlockSpec((1, TM, D), lambda b, i: (b, i, 0)), _mod_spec(layer, 1), _mod_spec(layer, 0)],
        out_specs=pl.BlockSpec((1, TM, D), lambda b, i: (b, i, 0)),
        out_shape=jax.ShapeDtypeStruct((NB, S, D), BF16),
        compiler_params=_params(2), name="modulate")(x, mod, mod)


def _swap32(x):
    n = x.shape[1]
    lane = lax.broadcasted_iota(I32, x.shape, 1)
    return jnp.where((lane & 32) == 0, pltpu.roll(x, n - 32, 1), pltpu.roll(x, 32, 1))


def _proj_kernel(a_ref, w_ref, cos_ref, sin_ref, o_ref, wbf_ref, *, rope_cols):
    @pl.when((pl.program_id(1) == 0) & (pl.program_id(2) == 0))
    def _():
        wbf_ref[...] = w_ref[...].astype(BF16)

    acc = jnp.dot(a_ref[0], wbf_ref[...], preferred_element_type=F32)
    if rope_cols:
        xr = acc[:, :rope_cols]
        reps = rope_cols // HD
        cos = jnp.concatenate([cos_ref[...]] * reps, axis=1)
        sin = jnp.concatenate([sin_ref[...]] * reps, axis=1)
        xr = xr * cos + _swap32(xr) * sin
        acc = xr if rope_cols == acc.shape[1] else jnp.concatenate([xr, acc[:, rope_cols:]], axis=1)
    o_ref[0] = acc.astype(o_ref.dtype)


def _proj(h, w, layer, col0, ncols, tn, rope_cols, rope, out_dtype, name):
    assert col0 % tn == 0 and ncols % tn == 0
    off = col0 // tn
    cos_t, sin_t = rope
    return pl.pallas_call(
        functools.partial(_proj_kernel, rope_cols=rope_cols),
        grid=(ncols // tn, NB, S // TMW),
        in_specs=[pl.BlockSpec((1, TMW, D), lambda j, b, i: (b, i, 0)),
                  pl.BlockSpec((None, D, tn), lambda j, b, i: (layer, 0, j + off)),
                  pl.BlockSpec((TMW, HD), lambda j, b, i: (i, 0)),
                  pl.BlockSpec((TMW, HD), lambda j, b, i: (i, 0))],
        out_specs=pl.BlockSpec((1, TMW, tn), lambda j, b, i: (b, i, j)),
        out_shape=jax.ShapeDtypeStruct((NB, S, ncols), out_dtype),
        scratch_shapes=[pltpu.VMEM((D, tn), BF16)],
        compiler_params=_params(3), name=name)(h, w, cos_t, sin_t)


def _rope_tables():
    t = jnp.arange(S)
    nf = HD // 4
    inv = ROPE_BASE ** (-2.0 * jnp.arange(nf, dtype=F32) / (HD // 2))
    lat = (t < L)[:, None]
    cos, sin = [], []
    for pos in (t // GW, t % GW):
        ang = pos.astype(F32)[:, None] * inv[None, :]
        c = jnp.where(lat, jnp.cos(ang), 1.0)
        s = jnp.where(lat, jnp.sin(ang), 0.0)
        cos += [c, c]
        sin += [-s, s]
    return jnp.concatenate(cos, axis=1), jnp.concatenate(sin, axis=1)


def _swa_kernel(sink_ref, q_ref, k0_ref, k1_ref, k2_ref, kc_ref, v0_ref, v1_ref, v2_ref, vc_ref, mask_ref, o_ref):
    mask = mask_ref[0]
    rg = lax.broadcasted_iota(I32, (SWA_G * SWA_W, 1), 0) // SWA_W
    for hk in range(SWA_KV):
        ks = slice(hk * HD, (hk + 1) * HD)
        qs = jnp.concatenate([q_ref[0, :, (hk * SWA_G + g) * HD:(hk * SWA_G + g + 1) * HD] for g in range(SWA_G)],
                             axis=0)
        k = jnp.concatenate([k0_ref[0, :, ks], k1_ref[0, :, ks], k2_ref[0, :, ks], kc_ref[0, :, ks]], axis=0)
        v = jnp.concatenate([v0_ref[0, :, ks], v1_ref[0, :, ks], v2_ref[0, :, ks], vc_ref[0, :, ks]], axis=0)
        s = lax.dot_general(qs, k, (((1,), (1,)), ((), ())), preferred_element_type=F32) * (HD ** -0.5 * LOG2E) + mask
        snk = jnp.zeros((SWA_G * SWA_W, 1), F32)
        for g in range(SWA_G):
            snk = jnp.where(rg == g, sink_ref[hk * SWA_G + g] * LOG2E, snk)
        m = jnp.maximum(jnp.max(s, axis=1, keepdims=True), snk)
        p = jnp.exp2(s - m)
        den = jnp.sum(p, axis=1, keepdims=True) + jnp.exp2(snk - m)
        o = jnp.dot(p.astype(BF16), v, preferred_element_type=F32) / den
        for g in range(SWA_G):
            o_ref[0, :, (hk * SWA_G + g) * HD:(hk * SWA_G + g + 1) * HD] = \
                o[g * SWA_W:(g + 1) * SWA_W].astype(o_ref.dtype)


def _swa_masks():
    r = jnp.arange(SWA_W)[:, None]
    c = jnp.arange(3 * SWA_W + LC)[None, :]
    band = (c - r >= 0) & (c - r <= 2 * SWA_W)
    is_ctx = c >= 3 * SWA_W
    variants = [band & (c >= SWA_W), band, band & (c < 2 * SWA_W), jnp.zeros_like(band)]
    m = jnp.stack([jnp.where(is_ctx | v, 0.0, NEG).astype(F32) for v in variants])
    return jnp.tile(m, (1, SWA_G, 1))


def _swa(q, kv, sink, masks):
    nb = L // SWA_W
    kvw = SWA_KV * HD

    def band(j, part):
        return pl.BlockSpec((1, SWA_W, kvw), lambda b, n: (b, jnp.clip(n + j - 1, 0, nb - 1), part))

    def ctx(part):
        return pl.BlockSpec((1, LC, kvw), lambda b, n: (b, L // LC, part))

    def variant(b, n):
        return (jnp.where(n == 0, 0, jnp.where(n < nb - 1, 1, jnp.where(n == nb - 1, 2, 3))), 0, 0)

    qspec = pl.BlockSpec((1, SWA_W, SWA_H * HD), lambda b, n: (b, n, 0))
    return pl.pallas_call(
        _swa_kernel, grid=(NB, S // SWA_W),
        in_specs=[pl.BlockSpec(memory_space=pltpu.SMEM), qspec,
                  band(0, 0), band(1, 0), band(2, 0), ctx(0), band(0, 1), band(1, 1), band(2, 1), ctx(1),
                  pl.BlockSpec((1,) + masks.shape[1:], variant)],
        out_specs=qspec,
        out_shape=jax.ShapeDtypeStruct((NB, S, SWA_H * HD), BF16),
        compiler_params=_params(2), name="swa")(sink, q, kv, kv, kv, kv, kv, kv, kv, kv, masks)


N_DR = 2 * NA_KH - 1
N_DC = 2 * NA_KW - 1
NA_G = 4
NA_WIN = NA_G + NA_KH
NA_QB = NA_G * GW
NA_KB = NA_WIN * GW // NA_QB


def _na_bias_kernel(rpb_ref, o_ref):
    h = pl.program_id(0)
    shape = (GW, 2 * GW)
    lane = lax.broadcasted_iota(I32, shape, 1)
    qc = lax.broadcasted_iota(I32, shape, 0)
    kc = lane & (GW - 1)
    diff = kc - qc + (NA_KW - 1)
    cs = jnp.clip(qc - NA_KW // 2, 0, GW - NA_KW)
    ok = (kc >= cs) & (kc < cs + NA_KW)
    right = lax.broadcasted_iota(I32, (1, 2 * GW), 1) >= GW
    for dd in range(N_DR + 1):
        acc = jnp.zeros(shape, F32)
        for m in range(N_DC):
            lo = rpb_ref[(h * N_DR + dd - 1) * N_DC + m] if dd >= 1 else 0.0
            hi = rpb_ref[(h * N_DR + dd) * N_DC + m] if dd < N_DR else 0.0
            acc = jnp.where(diff == m, jnp.where(right, hi, lo), acc)
        valid = ok & (right if dd == 0 else (~right if dd == N_DR else True))
        o_ref[0, dd] = jnp.where(valid, acc * LOG2E, NEG)


def _na_bias(rpb):
    return pl.pallas_call(
        _na_bias_kernel, grid=(NA_H,),
        in_specs=[pl.BlockSpec(memory_space=pltpu.SMEM)],
        out_specs=pl.BlockSpec((1, N_DR + 1, GW, 2 * GW), lambda h: (h, 0, 0, 0)),
        out_shape=jax.ShapeDtypeStruct((NA_H, N_DR + 1, GW, 2 * GW), F32),
        compiler_params=_params(1), name="na_bias")(rpb.reshape(-1))


def _na_kernel(*refs):
    q_ref = refs[0]
    k_refs = refs[1:1 + NA_KB]
    v_refs = refs[1 + NA_KB:1 + 2 * NA_KB]
    kc_ref, vc_ref, tz_ref, o_ref = refs[1 + 2 * NA_KB:]
    scale = HD ** -0.5 * LOG2E
    nt = (((1,), (1,)), ((), ()))
    rows = L // GW
    g = pl.program_id(1)
    is_lat = g < rows // NA_G
    r0 = g * NA_G
    ustart = jnp.clip(r0 - NA_KH // 2, 0, rows - NA_WIN)
    lane = lax.broadcasted_iota(I32, (1, NA_WIN * GW), 1)
    masks, dds = [], []
    for a in range(NA_G):
        r = r0 + a
        lo = (jnp.clip(r - NA_KH // 2, 0, rows - NA_KH) - ustart) * GW
        hi = jnp.where(is_lat, lo + NA_KH * GW, lo)
        masks.append((lane >= lo) & (lane < hi))
        dds.append([jnp.clip(ustart + 2 * p - r + NA_KH, 0, N_DR) for p in range(NA_WIN // 2)])
    for h in range(NA_H):
        hs = slice(h * HD, (h + 1) * HD)
        qh = q_ref[0, :, hs]
        kh = jnp.concatenate([r[0, :, hs] for r in k_refs], axis=0)
        vh = jnp.concatenate([r[0, :, hs] for r in v_refs], axis=0)
        s = lax.dot_general(qh, kh, nt, preferred_element_type=F32) * scale
        parts = []
        for a in range(NA_G):
            bias = jnp.concatenate([tz_ref[h, dd] for dd in dds[a]], axis=1)
            parts.append(jnp.where(masks[a], s[a * GW:(a + 1) * GW] + bias, NEG))
        s = jnp.concatenate(parts, axis=0)
        sc = lax.dot_general(qh, kc_ref[0, :, hs], nt, preferred_element_type=F32) * scale
        m = jnp.maximum(jnp.max(s, axis=1, keepdims=True), jnp.max(sc, axis=1, keepdims=True))
        p = jnp.exp2(s - m)
        pc = jnp.exp2(sc - m)
        den = jnp.sum(p, axis=1, keepdims=True) + jnp.sum(pc, axis=1, keepdims=True)
        o = jnp.dot(p.astype(BF16), vh, preferred_element_type=F32)
        o = o + jnp.dot(pc.astype(BF16), vc_ref[0, :, hs], preferred_element_type=F32)
        o_ref[0, :, hs] = (o / den).astype(o_ref.dtype)


def _na(qkv, tz):
    rows = L // GW
    assert LC == NA_QB and (rows - NA_WIN) % NA_G == 0

    def win(j, part):
        return pl.BlockSpec(
            (1, NA_QB, D), lambda b, g: (b, jnp.clip(g * NA_G - NA_KH // 2, 0, rows - NA_WIN) // NA_G + j, part))

    in_specs = [pl.BlockSpec((1, NA_QB, D), lambda b, g: (b, g, 0))]
    in_specs += [win(j, 1) for j in range(NA_KB)] + [win(j, 2) for j in range(NA_KB)]
    in_specs += [pl.BlockSpec((1, LC, D), lambda b, g: (b, L // LC, 1)),
                 pl.BlockSpec((1, LC, D), lambda b, g: (b, L // LC, 2)),
                 pl.BlockSpec(tz.shape, lambda b, g: (0, 0, 0, 0))]
    return pl.pallas_call(
        _na_kernel, grid=(NB, S // NA_QB),
        in_specs=in_specs,
        out_specs=pl.BlockSpec((1, NA_QB, D), lambda b, g: (b, g, 0)),
        out_shape=jax.ShapeDtypeStruct((NB, S, D), BF16),
        compiler_params=_params(2), name="na")(*([qkv] * (3 + 2 * NA_KB)), tz)


def _filter_kernel(z_ref, w1_ref, b1_ref, w2_ref, b2_ref, w3_ref, fr_ref, dl_ref, o_ref):
    hp = lax.Precision.HIGHEST
    z = z_ref[...]
    fr = fr_ref[...]
    h = jnp.sin(fr * (jnp.dot(z, w1_ref[...], precision=hp, preferred_element_type=F32) + b1_ref[...]))
    for i in range(w2_ref.shape[0]):
        h = jnp.sin(fr * (jnp.dot(h, w2_ref[i], precision=hp, preferred_element_type=F32) + b2_ref[i]))
    h = jnp.dot(h.astype(BF16), w3_ref[...].astype(BF16), preferred_element_type=F32)
    decay = jnp.exp(-z[:, 0:1] * dl_ref[...])
    parts = []
    for order in range(2):
        fwd = h[:, (2 * order) * HY:(2 * order + 1) * HY] * decay
        bwd = h[:, (2 * order + 1) * HY:(2 * order + 2) * HY] * decay
        parts += [fwd + bwd, fwd - bwd]
    o_ref[...] = jnp.concatenate(parts, axis=1).astype(o_ref.dtype)


def _filter_embedding(n):
    t = jnp.linspace(0.0, 1.0, n, dtype=F32)[:, None]
    w = (2.0 * math.pi / n) * jnp.arange(n, dtype=F32)[:, None]
    f = jnp.linspace(1e-4, HY_BANDS - 1, HY_BANDS, dtype=F32)[None, :]
    z = jnp.concatenate([t, jnp.cos(f * w), -jnp.sin(f * w)], axis=-1)
    return jnp.pad(z, ((0, 0), (0, HD - HY_EMB)))


def _hyena_filters(n, fw1, fb1, fw2, fb2, fw3, ffreq):
    deltas = jnp.abs(jnp.linspace(math.log(HY_TARGET) / HY_SLOW_DECAY, math.log(HY_TARGET) / HY_FAST_DECAY,
                                  HY, dtype=F32))[None, :]
    w1 = jnp.pad(fw1, ((0, HD - HY_EMB), (0, 0)))
    tm = min(n, 256)
    nout = fw3.shape[1]
    full = lambda a: pl.BlockSpec(a.shape, lambda i: (0,) * a.ndim)
    args = (_filter_embedding(n), w1, fb1.reshape(1, -1), fw2, fb2.reshape(fb2.shape[0], 1, -1), fw3,
            ffreq.reshape(1, -1), deltas)
    return pl.pallas_call(
        _filter_kernel, grid=(n // tm,),
        in_specs=[pl.BlockSpec((tm, HD), lambda i: (i, 0))] + [full(a) for a in args[1:]],
        out_specs=pl.BlockSpec((tm, nout), lambda i: (i, 0)),
        out_shape=jax.ShapeDtypeStruct((n, nout), BF16),
        compiler_params=_params(1), name="hy_filter")(*args)


def _shortconv_kernel(u_ref, w_ref, b_ref, o_ref):
    u = u_ref[0].astype(F32)
    t = lax.broadcasted_iota(I32, (S, 1), 0)
    prev = jnp.where((t == 0) | (t == L), 0.0, pltpu.roll(u, 1, 0))
    nxt = jnp.where((t == L - 1) | (t == S - 1), 0.0, pltpu.roll(u, S - 1, 0))
    w = w_ref[...]
    o_ref[0] = (b_ref[...] + prev * w[0:1] + u * w[1:2] + nxt * w[2:3]).astype(o_ref.dtype)


def _shortconv(u, w, b):
    tc = 256
    nc = u.shape[2]
    return pl.pallas_call(
        _shortconv_kernel, grid=(NB, nc // tc),
        in_specs=[pl.BlockSpec((1, S, tc), lambda b_, c: (b_, 0, c)),
                  pl.BlockSpec((3, tc), lambda b_, c: (0, c)),
                  pl.BlockSpec((1, tc), lambda b_, c: (0, c))],
        out_specs=pl.BlockSpec((1, S, tc), lambda b_, c: (b_, 0, c)),
        out_shape=jax.ShapeDtypeStruct(u.shape, BF16),
        compiler_params=_params(2), name="shortconv")(u, w, b.reshape(1, -1))


def _dft_tables(n):
    nn = 2 * n
    lo = 64
    f = jnp.arange(n, dtype=I32)[:, None]
    angle = lambda m: (m % nn).astype(F32) * (2.0 * math.pi / nn)
    a = angle(f * (jnp.arange(n // lo, dtype=I32) * lo)[None, :])[:, :, None]
    b = angle(f * jnp.arange(lo, dtype=I32)[None, :])[:, None, :]
    c = (jnp.cos(a) * jnp.cos(b) - jnp.sin(a) * jnp.sin(b)).reshape(n, n)
    s = -(jnp.sin(a) * jnp.cos(b) + jnp.cos(a) * jnp.sin(b)).reshape(n, n)
    i = jnp.arange(n, dtype=I32)
    alt = jnp.where(i % 2 == 0, 1.0, -1.0).astype(F32)
    fwd = jnp.concatenate([c, jnp.where(i[:, None] == 0, alt[None, :], s)], axis=0)
    inv_r = c * jnp.where(i == 0, 1.0 / nn, 2.0 / nn).astype(F32)[None, :]
    inv_i = jnp.where(i[None, :] == 0, alt[:, None] / nn, s * (2.0 / nn))
    return fwd.astype(BF16), inv_r.astype(BF16), inv_i.astype(BF16)


HY_TN = 512
HY_NCT = HY // HY_TN


def _filt_dft_kernel(c_ref, s_ref, hs_ref, hd_ref, kr_ref, ki_ref):
    kr_ref[0] = jnp.dot(c_ref[...], hs_ref[...], preferred_element_type=F32)
    ki_ref[0] = jnp.dot(s_ref[...], hd_ref[...], preferred_element_type=F32)

    @pl.when(pl.program_id(0) == 0)
    def _():
        nyq = jnp.dot(s_ref[0:8, :], hs_ref[...], preferred_element_type=F32)
        first = lax.broadcasted_iota(I32, nyq.shape, 0) == 0
        ki_ref[0, 0:8, :] = jnp.where(first, nyq, ki_ref[0, 0:8, :])


def _filt_dft(fwd, hfilt, n):
    tf = min(n, 1024)
    nf = n // tf
    out = pl.BlockSpec((1, tf, HY_TN), lambda f, c: (c // HY_NCT, f, c % HY_NCT))
    return pl.pallas_call(
        _filt_dft_kernel, grid=(nf, 2 * HY_NCT),
        in_specs=[pl.BlockSpec((tf, n), lambda f, c: (f, 0)),
                  pl.BlockSpec((tf, n), lambda f, c: (nf + f, 0)),
                  pl.BlockSpec((n, HY_TN), lambda f, c: (0, (c // HY_NCT) * 2 * HY_NCT + c % HY_NCT)),
                  pl.BlockSpec((n, HY_TN), lambda f, c: (0, (c // HY_NCT) * 2 * HY_NCT + HY_NCT + c % HY_NCT))],
        out_specs=[out, out],
        out_shape=[jax.ShapeDtypeStruct((2, n, HY), F32)] * 2,
        compiler_params=_params(2), name=f"filt_dft{n}")(fwd, fwd, hfilt, hfilt)


def _dft_mul_kernel(z_ref, c_ref, s_ref, kr_ref, ki_ref, yr_ref, yi_ref):
    z = z_ref[0].astype(BF16)
    zr = jnp.dot(c_ref[...], z, preferred_element_type=F32)
    zi = jnp.dot(s_ref[...], z, preferred_element_type=F32)
    kr = kr_ref[...]
    ki = ki_ref[...]
    first = (lax.broadcasted_iota(I32, zr.shape, 0) == 0) & (pl.program_id(0) == 0)
    ii = zi * ki
    yr_ref[0] = (zr * kr - jnp.where(first, 0.0, ii)).astype(yr_ref.dtype)
    yi_ref[0] = jnp.where(first, ii, zr * ki + zi * kr).astype(yi_ref.dtype)


def _dft_mul(z, zrow, zc0, n, fwd, kr, ki, order):
    tf = min(n, 1024)
    nf = n // tf
    out = pl.BlockSpec((1, tf, HY_TN), lambda f, b, c: (b, f, c))
    kspec = lambda: pl.BlockSpec((None, tf, HY_TN), lambda f, b, c: (order, f, c))
    return pl.pallas_call(
        _dft_mul_kernel, grid=(nf, NB, HY_NCT),
        in_specs=[pl.BlockSpec((1, n, HY_TN), lambda f, b, c: (b, zrow, zc0 + c)),
                  pl.BlockSpec((tf, n), lambda f, b, c: (f, 0)),
                  pl.BlockSpec((tf, n), lambda f, b, c: (nf + f, 0)),
                  kspec(), kspec()],
        out_specs=[out, out],
        out_shape=[jax.ShapeDtypeStruct((NB, n, HY), BF16)] * 2,
        compiler_params=_params(3), name=f"dft_mul{n}")(z, fwd, fwd, kr, ki)


def _idft_gate_kernel(ar_ref, ai_ref, yr_ref, yi_ref, z_ref, p_ref, bias_ref, *o_refs):
    y = jnp.dot(ar_ref[...], yr_ref[0], preferred_element_type=F32)
    y = y + jnp.dot(ai_ref[...], yi_ref[0], preferred_element_type=F32)
    out = p_ref[0].astype(F32) * (y + z_ref[0].astype(F32) * bias_ref[0])
    for o_ref in o_refs:
        o_ref[0] = out.astype(o_ref.dtype)


def _idft_gate(yr, yi, inv_r, inv_i, n, z, z_r0, z_c0, p, p_r0, p_c0, bias, order, out_dtype):
    tm = min(n, 1024)
    return pl.pallas_call(
        _idft_gate_kernel, grid=(n // tm, NB, HY_NCT),
        in_specs=[pl.BlockSpec((tm, n), lambda t, b, c: (t, 0)),
                  pl.BlockSpec((tm, n), lambda t, b, c: (t, 0)),
                  pl.BlockSpec((1, n, HY_TN), lambda t, b, c: (b, 0, c)),
                  pl.BlockSpec((1, n, HY_TN), lambda t, b, c: (b, 0, c)),
                  pl.BlockSpec((1, tm, HY_TN), lambda t, b, c: (b, z_r0 + t, z_c0 + c)),
                  pl.BlockSpec((1, tm, HY_TN), lambda t, b, c: (b, p_r0 + t, p_c0 + c)),
                  pl.BlockSpec((1, 1, HY_TN), lambda t, b, c: (order, 0, c))],
        out_specs=pl.BlockSpec((1, tm, HY_TN), lambda t, b, c: (b, t, c)),
        out_shape=jax.ShapeDtypeStruct((NB, n, HY), out_dtype),
        compiler_params=_params(3), name=f"idft_gate{n}_{order}")(inv_r, inv_i, yr, yi, z, p, bias)


def _hyena(u, conv_w, conv_b, filt, hy_bias, tables):
    sc = _shortconv(u, conv_w, conv_b)
    bias = hy_bias.reshape(2, 1, HY)
    outs = []
    for n in (L, LC):
        zrow = 0 if n == L else L // LC
        r0 = zrow * (n // min(n, 1024))
        fwd, inv_r, inv_i = tables[n]
        kr, ki = _filt_dft(fwd, _hyena_filters(n, *filt), n)
        yr, yi = _dft_mul(sc, zrow, 2 * HY_NCT, n, fwd, kr, ki, 0)
        z1 = _idft_gate(yr, yi, inv_r, inv_i, n, sc, r0, 2 * HY_NCT, sc, r0, 0, bias, 0, BF16)
        yr, yi = _dft_mul(z1, 0, 0, n, fwd, kr, ki, 1)
        outs.append(_idft_gate(yr, yi, inv_r, inv_i, n, z1, 0, 0, sc, r0, HY_NCT, bias, 1, BF16))
    return outs


def _resid_ln(x, y, gate, g, b):
    v = DN_ALPHA * x + gate * y
    mu = jnp.mean(v, axis=-1, keepdims=True)
    vc = v - mu
    var = jnp.mean(vc * vc, axis=-1, keepdims=True)
    return vc * lax.rsqrt(var + LN_EPS) * g + b


def _wout_ln_kernel(a1a_ref, a1b_ref, a1c_ref, a1x_ref, a2_ref, w_ref, x_ref, gl_ref, gc_ref, g_ref, b_ref,
                    scl_ref, scc_ref, shl_ref, shc_ref, wrh_ref, wrl_ref, xo_ref, hm_ref, aff_ref):
    half = a2_ref.shape[2]
    last_tile = pl.program_id(1) == pl.num_programs(1) - 1
    a1_refs = (a1a_ref, a1b_ref, a1c_ref)
    n_sub = len(a1_refs)

    def project(s):
        a1 = jnp.where(last_tile, a1x_ref[0], a1_refs[s][0]) if s == n_sub - 1 else a1_refs[s][0]
        y = jnp.dot(a1, w_ref[0:half, :], preferred_element_type=F32)
        return y + jnp.dot(a2_ref[0, s * TM:(s + 1) * TM, :], w_ref[half:2 * half, :], preferred_element_type=F32)

    for s in range(n_sub):
        rows = slice(s * TM, (s + 1) * TM)
        if s == n_sub - 1:
            pick = lambda lat_ref, ctx_ref: jnp.where(last_tile, ctx_ref[0], lat_ref[0])
        else:
            pick = lambda lat_ref, ctx_ref: lat_ref[0]
        xn = _resid_ln(x_ref[0, rows, :], project(s), pick(gl_ref, gc_ref), g_ref[0], b_ref[0])
        xo_ref[0, rows, :] = xn
        hm = xn * (1.0 + pick(scl_ref, scc_ref)) + pick(shl_ref, shc_ref)
        hm_hi = hm.astype(BF16)
        hm_ref[0, rows, :] = hm_hi
        hm_lo = (hm - hm_hi.astype(F32)).astype(BF16)
        logits = jnp.dot(hm_hi, wrh_ref[...], preferred_element_type=F32)
        logits = logits + jnp.dot(hm_lo, wrh_ref[...], preferred_element_type=F32)
        logits = logits + jnp.dot(hm_hi, wrl_ref[...], preferred_element_type=F32)
        e = jnp.exp(logits - jnp.max(logits, axis=-1, keepdims=True))
        aff_ref[0, rows, :] = e / jnp.sum(e, axis=-1, keepdims=True)


def _wout_ln(a1_lat, a1_ctx, a1_ctx_blk, c1, a2, c2, w_bf, x, mod, layer, lng, lnb, w_router):
    half = D // 2
    sub = TMW // TM
    n_lat_blk = a1_lat.shape[1] // TM
    tile = lambda: pl.BlockSpec((1, TMW, D), lambda b, i: (b, i, 0))
    vec = lambda idx: pl.BlockSpec((1, 1, D), idx)
    lat = lambda k: vec(lambda b, i: ((layer * 8 + b) * 6 + k, 0, 0))
    ctx = lambda k: vec(lambda b, i: ((layer * 8 + NB) * 6 + k, 0, 0))
    a1_blk = lambda s: pl.BlockSpec((1, TM, half), lambda b, i: (b, jnp.minimum(i * sub + s, n_lat_blk - 1), c1))
    wr_hi = w_router.astype(BF16)
    wr_lo = (w_router - wr_hi.astype(F32)).astype(BF16)
    return pl.pallas_call(
        _wout_ln_kernel, grid=(NB, S // TMW),
        in_specs=[a1_blk(0), a1_blk(1), a1_blk(2),
                  pl.BlockSpec((1, TM, half), lambda b, i: (b, a1_ctx_blk, c1)),
                  pl.BlockSpec((1, TMW, half), lambda b, i: (b, i, c2)),
                  pl.BlockSpec((D, D), lambda b, i: (0, 0), pipeline_mode=pl.Buffered(1)),
                  tile(), lat(2), ctx(2),
                  vec(lambda b, i: (2 * layer, 0, 0)), vec(lambda b, i: (2 * layer, 0, 0)),
                  lat(4), ctx(4), lat(3), ctx(3),
                  pl.BlockSpec((D, NE), lambda b, i: (0, 0)),
                  pl.BlockSpec((D, NE), lambda b, i: (0, 0))],
        out_specs=[tile(), tile(), pl.BlockSpec((1, TMW, NE), lambda b, i: (b, i, 0))],
        out_shape=[jax.ShapeDtypeStruct((NB, S, D), F32), jax.ShapeDtypeStruct((NB, S, D), BF16),
                   jax.ShapeDtypeStruct((NB, S, NE), F32)],
        compiler_params=_params(2), name="wout_ln")(
            a1_lat, a1_lat, a1_lat, a1_ctx, a2, w_bf, x, mod, mod, lng, lnb, mod, mod, mod, mod, wr_hi, wr_lo)


def _route_kernel(a_ref, o_ref, tri_ref):
    rb = 256
    for r in range(L // rb):
        s_i = lax.broadcasted_iota(I32, (rb, L), 0) + r * rb
        t_i = lax.broadcasted_iota(I32, (rb, L), 1)
        tri_ref[r * rb:(r + 1) * rb, :] = jnp.where(s_i < t_i, 1.0, 0.0).astype(BF16)

    def select(a, cap, base):
        n = a.shape[1]

        def body(i, prefix):
            cand = prefix | jnp.left_shift(jnp.int32(1), 30 - i)
            cnt = jnp.sum(jnp.where(a >= lax.bitcast_convert_type(cand, F32), 1.0, 0.0), axis=1, keepdims=True)
            return jnp.where(cnt >= cap, cand, prefix)

        kth = lax.bitcast_convert_type(lax.fori_loop(0, 31, body, jnp.zeros((a.shape[0], 1), I32)), F32)
        gt = a > kth
        eq = a == kth
        n_gt = jnp.sum(jnp.where(gt, 1.0, 0.0), axis=1, keepdims=True)
        tri = tri_ref[0:n, 0:n]
        before = jnp.dot(jnp.where(eq, 1.0, 0.0).astype(BF16), tri, preferred_element_type=F32)
        sel = gt | (eq & (before < cap - n_gt))
        pos = jnp.dot(jnp.where(sel, 1.0, 0.0).astype(BF16), tri, preferred_element_type=F32)
        return jnp.where(sel, pos.astype(I32) + base, -1)

    a = a_ref[...]
    o_ref[...] = jnp.concatenate([select(a[:, :L], CAP_L, 0), select(a[:, L:], CAP_C, CAP_L)], axis=1)


def _route(aff_t):
    return pl.pallas_call(
        _route_kernel, grid=(1,),
        in_specs=[pl.BlockSpec((NB * NE, S), lambda i: (0, 0))],
        out_specs=pl.BlockSpec((NB * NE, S), lambda i: (0, 0)),
        out_shape=jax.ShapeDtypeStruct((NB * NE, S), I32),
        scratch_shapes=[pltpu.VMEM((L, L), BF16)],
        compiler_params=_params(1), name="route")(aff_t.reshape(NB * NE, S)).reshape(NB, NE, S)


GATHER_EG = 4


def _gather_kernel(slot_ref, aff_ref, h_ref, xs_ref, g_ref):
    for r0, cap, c0, n in ((0, CAP_L, 0, L), (CAP_L, CAP_C, L, LC)):
        want = lax.broadcasted_iota(I32, (cap, n), 0) + r0
        hits = []
        for e in range(GATHER_EG):
            hit = slot_ref[0, e][:, c0:c0 + n] == want
            hits.append(jnp.where(hit, 1.0, 0.0).astype(BF16))
            g = jnp.sum(jnp.where(hit, aff_ref[0, e][:, c0:c0 + n], 0.0), axis=1, keepdims=True)
            g_ref[0, e, r0:r0 + cap, :] = jnp.broadcast_to(g, (cap, HD))
        xs = jnp.dot(jnp.concatenate(hits, axis=0), h_ref[0, c0:c0 + n, :], preferred_element_type=F32)
        for e in range(GATHER_EG):
            xs_ref[0, e, r0:r0 + cap, :] = xs[e * cap:(e + 1) * cap].astype(xs_ref.dtype)


def _gather(slot, aff_t, hm):
    row = lambda: pl.BlockSpec((1, GATHER_EG, 1, S), lambda b, e: (b, e, 0, 0))
    return pl.pallas_call(
        _gather_kernel, grid=(NB, NE // GATHER_EG),
        in_specs=[row(), row(), pl.BlockSpec((1, S, D), lambda b, e: (b, 0, 0))],
        out_specs=[pl.BlockSpec((1, GATHER_EG, CAP, D), lambda b, e: (b, e, 0, 0)),
                   pl.BlockSpec((1, GATHER_EG, CAP, HD), lambda b, e: (b, e, 0, 0))],
        out_shape=[jax.ShapeDtypeStruct((NB, NE, CAP, D), BF16), jax.ShapeDtypeStruct((NB, NE, CAP, HD), F32)],
        compiler_params=_params(2), name="moe_gather")(
            slot.reshape(NB, NE, 1, S), aff_t.reshape(NB, NE, 1, S), hm)


FFN_SPLIT = 2


def _ffn_kernel(xs_ref, g_ref, w1_ref, w3_ref, w2_ref, ol_ref, oc_ref, hmid_ref):
    f = pl.program_id(1)
    tf = FF // FFN_SPLIT

    @pl.when(f < FFN_SPLIT)
    def _():
        x = xs_ref[:, 0].reshape(NB * CAP, D)
        a = jnp.dot(x, w1_ref[0].astype(BF16), preferred_element_type=F32)
        u = jnp.dot(x, w3_ref[0].astype(BF16), preferred_element_type=F32)
        hmid_ref[:, pl.ds(pl.multiple_of(f * tf, tf), tf)] = (a * jax.nn.sigmoid(a) * u).astype(BF16)

    @pl.when(f >= FFN_SPLIT)
    def _():
        y = jnp.dot(hmid_ref[...], w2_ref[0].astype(BF16), preferred_element_type=F32)
        g = g_ref[:, 0].reshape(NB * CAP, HD)[:, 0:1]
        y = (y * g).astype(ol_ref.dtype).reshape(NB, CAP, y.shape[1])
        ol_ref[:, 0] = y[:, :CAP_L]
        oc_ref[:, 0] = y[:, CAP_L:]


def _ffn(xs, gate, w1, w3, w2, layer):
    tf = FF // FFN_SPLIT
    tn = D // FFN_SPLIT
    hid = lambda e, f: (layer, e, 0, jnp.minimum(f, FFN_SPLIT - 1))
    col = lambda f: jnp.maximum(f - FFN_SPLIT, 0)
    out = lambda cap: pl.BlockSpec((NB, 1, cap, tn), lambda e, f: (0, e, 0, col(f)))

    def rows_index(e, f):
        return (0, jnp.where(f < FFN_SPLIT, e, jnp.minimum(e + 1, NE - 1)), 0, 0)

    def w2_index(e, f):
        hidden = f < FFN_SPLIT
        return (layer, jnp.where(hidden, jnp.maximum(e - 1, 0), e), 0, jnp.where(hidden, FFN_SPLIT - 1, col(f)))

    return pl.pallas_call(
        _ffn_kernel, grid=(NE, 2 * FFN_SPLIT),
        in_specs=[pl.BlockSpec((NB, 1, CAP, D), rows_index),
                  pl.BlockSpec((NB, 1, CAP, HD), lambda e, f: (0, e, 0, 0)),
                  pl.BlockSpec((None, 1, D, tf), hid),
                  pl.BlockSpec((None, 1, D, tf), hid),
                  pl.BlockSpec((None, 1, FF, tn), w2_index)],
        out_specs=[out(CAP_L), out(CAP_C)],
        out_shape=[jax.ShapeDtypeStruct((NB, NE, CAP_L, D), BF16), jax.ShapeDtypeStruct((NB, NE, CAP_C, D), BF16)],
        scratch_shapes=[pltpu.VMEM((NB * CAP, FF), BF16)],
        compiler_params=_params(2), name="moe_ffn")(xs, gate, w1, w3, w2)


def _combine_ln_kernel(*refs, last):
    if last:
        slot_ref, yl_ref, x_ref, gate_ref, g_ref, b_ref, xo_ref, acc_ref = refs
    else:
        slot_ref, yl_ref, yc_ref, x_ref, gate_ref, g_ref, b_ref, sc_ref, sh_ref, xo_ref, hn_ref, acc_ref = refs
    st = slot_ref[0]

    def scatter(y_ref, r0, cap):
        want = lax.broadcasted_iota(I32, (TM, cap), 1) + r0
        hits = [jnp.where(st[:, e:e + 1] == want, 1.0, 0.0).astype(BF16) for e in range(NE)]
        if cap % HD == 0:
            acc_ref[...] = jnp.dot(jnp.concatenate(hits, axis=1), y_ref[0].reshape(NE * cap, D),
                                   preferred_element_type=F32)
        else:
            acc = None
            for e in range(NE):
                part = jnp.dot(hits[e], y_ref[0, e], preferred_element_type=F32)
                acc = part if acc is None else acc + part
            acc_ref[...] = acc

    if last:
        scatter(yl_ref, 0, CAP_L)
    else:
        is_lat = pl.program_id(1) < N_LAT_TILES
        pl.when(is_lat)(lambda: scatter(yl_ref, 0, CAP_L))
        pl.when(jnp.logical_not(is_lat))(lambda: scatter(yc_ref, CAP_L, CAP_C))
    xn = _resid_ln(x_ref[0], acc_ref[...], gate_ref[0], g_ref[0], b_ref[0])
    xo_ref[0] = xn
    if not last:
        hn_ref[0] = (xn * (1.0 + sc_ref[0]) + sh_ref[0]).astype(hn_ref.dtype)


def _combine_ln(slot_t, yl, yc, x, mod, layer, lng, lnb, last):
    rows = L if last else S
    tile = lambda: pl.BlockSpec((1, TM, D), lambda b, i: (b, i, 0))
    ln_vec = lambda: pl.BlockSpec((1, 1, D), lambda b, i: (2 * layer + 1, 0, 0))
    in_specs = [pl.BlockSpec((1, TM, NE), lambda b, i: (b, i, 0)),
                pl.BlockSpec((1, NE, CAP_L, D), lambda b, i: (b, 0, 0, 0), pipeline_mode=pl.Buffered(1))]
    args = [slot_t, yl]
    if not last:
        in_specs.append(pl.BlockSpec((1, NE, CAP_C, D), lambda b, i: (b, 0, 0, 0)))
        args.append(yc)
    in_specs += [tile(), _mod_spec(layer, 5), ln_vec(), ln_vec()]
    args += [x, mod, lng, lnb]
    out_specs = [tile()]
    out_shape = [jax.ShapeDtypeStruct((NB, rows, D), F32)]
    if not last:
        in_specs += [_mod_spec(layer + 1, 1), _mod_spec(layer + 1, 0)]
        args += [mod, mod]
        out_specs.append(tile())
        out_shape.append(jax.ShapeDtypeStruct((NB, rows, D), BF16))
    return pl.pallas_call(
        functools.partial(_combine_ln_kernel, last=last), grid=(NB, rows // TM),
        in_specs=in_specs, out_specs=out_specs, out_shape=out_shape,
        scratch_shapes=[pltpu.VMEM((TM, D), F32)],
        compiler_params=_params(2), name="moe_combine_ln")(*args)


def _moe(hm, aff, x, mod, layer, lng, lnb, w1, w3, w2, last):
    aff_t = jnp.swapaxes(aff, 1, 2)
    slot = _route(aff_t)
    xs, gate = _gather(slot, aff_t, hm)
    yl, yc = _ffn(xs, gate, w1, w3, w2, layer)
    return _combine_ln(jnp.swapaxes(slot, 1, 2), yl, yc, x, mod, layer, lng, lnb, last)


def kernel(x, c, ctx, c_ctx, ada_w, ada_b, ln_g, ln_b, ev_w_in, ev_w_out, hy_conv_w, hy_conv_b, hy_f_w1, hy_f_b1,
           hy_f_w2, hy_f_b2, hy_f_w3, hy_f_freq, hy_bias, swa_sink, od_w_in, od_w_out, na_rpb, moe_w_router,
           moe_w1, moe_w3, moe_w2):
    assert x.shape == (NB, L, D) and ctx.shape == (NB, LC, D)
    xa = jnp.concatenate([x, ctx], axis=1)
    cc = jnp.concatenate([c, c_ctx[None, :], jnp.zeros((8 - NB - 1, D), F32)], axis=0)
    mod = _ada(cc, ada_w, ada_b).reshape(DEPTH * 8 * 6, 1, D)
    lng = ln_g.reshape(DEPTH * 2, 1, D)
    lnb = ln_b.reshape(DEPTH * 2, 1, D)
    rope = _rope_tables()
    tables = {L: _dft_tables(L), LC: _dft_tables(LC)}
    swa_masks = _swa_masks()
    h = _modulate(xa, mod, 0)
    for l in range(DEPTH):
        i = l // 2
        last = l == DEPTH - 1
        if l % 2 == 0:
            u = _proj(h, ev_w_in, i, 0, EV_Q0, 1024, 0, rope, BF16, "proj_hy")
            q = _proj(h, ev_w_in, i, EV_Q0, SWA_H * HD, 1024, SWA_H * HD, rope, BF16, "proj_q")
            kv = _proj(h, ev_w_in, i, EV_K0, 2 * SWA_KV * HD, 512, SWA_KV * HD, rope, BF16, "proj_kv")
            att = _swa(q, kv, swa_sink[i], swa_masks)
            filt = (hy_f_w1[i], hy_f_b1[i], hy_f_w2[i], hy_f_b2[i], hy_f_w3[i], hy_f_freq[i])
            z_lat, z_ctx = _hyena(u, hy_conv_w[i], hy_conv_b[i], filt, hy_bias[i], tables)
            xa, hm, aff = _wout_ln(z_lat, z_ctx, 0, 0, att, 0, ev_w_out[i].astype(BF16), xa, mod, l, lng, lnb,
                                   moe_w_router[l])
        else:
            qkv = _proj(h, od_w_in, i, 0, 3 * D, 1024, 0, rope, BF16, "proj_qkv")
            att = _na(qkv, _na_bias(na_rpb[i]))
            xa, hm, aff = _wout_ln(att, att, L // TM, 0, att, 1, od_w_out[i].astype(BF16), xa, mod, l, lng, lnb,
                                   moe_w_router[l])
        out = _moe(hm, aff, xa, mod, l, lng, lnb, moe_w1, moe_w3, moe_w2, last)
        if last:
            return out[0]
        xa, h = out
```

```python
import functools
import math

import jax
import jax.numpy as jnp
from jax import lax
from jax.experimental import pallas as pl
from jax.experimental.pallas import tpu as pltpu

F32 = jnp.float32
BF16 = jnp.bfloat16
I32 = jnp.int32

D = 2048
NB = 4
L = 2048
LC = 256
S = L + LC
DEPTH = 4
GW = 64
HD = 128
HY = D // 2
HY_BANDS = 16
HY_EMB = 2 * HY_BANDS + 1
HY_HID = 64
HY_FAST_DECAY = 0.3
HY_SLOW_DECAY = 1.5
HY_TARGET = 1e-2
SWA_H = (D - HY) // HD
SWA_KV = SWA_H // 4
SWA_G = SWA_H // SWA_KV
SWA_W = 128
NA_H = D // HD
NA_KH = 8
NA_KW = 16
ROPE_BASE = 10000.0
NE = 16
CAP_L = 2 * L // NE
CAP_C = 2 * LC // NE
CAP = CAP_L + CAP_C
FF = D // 2
DN_ALPHA = (2 * DEPTH) ** 0.25
LN_EPS = 1e-5
EV_Q0 = 3 * HY
EV_K0 = EV_Q0 + SWA_H * HD
EV_IN = EV_K0 + 2 * SWA_KV * HD
NEG = -1e30
LOG2E = math.log2(math.e)

TM = 256
N_LAT_TILES = L // TM
TMW = 3 * TM
VMEM_LIMIT = 56 * 1024 * 1024


def _params(n_axes, vmem=VMEM_LIMIT):
    return pltpu.CompilerParams(dimension_semantics=("arbitrary",) * n_axes, vmem_limit_bytes=vmem)


def _mod_spec(layer, k):
    def index(b, i):
        row = jnp.where(i >= N_LAT_TILES, NB, b)
        return ((layer * 8 + row) * 6 + k, 0, 0)
    return pl.BlockSpec((1, 1, D), index)


def _ada_kernel(c_ref, w_ref, b_ref, o_ref):
    c = c_ref[...]
    s = c * jax.nn.sigmoid(c)
    o_ref[0] = jnp.dot(s.astype(BF16), w_ref[0].astype(BF16), preferred_element_type=F32) + b_ref[0]


def _ada(cc, ada_w, ada_b):
    tn = 1024
    return pl.pallas_call(
        _ada_kernel, grid=(DEPTH, 6 * D // tn),
        in_specs=[pl.BlockSpec((8, D), lambda l, j: (0, 0)),
                  pl.BlockSpec((1, D, tn), lambda l, j: (l, 0, j)),
                  pl.BlockSpec((1, 1, tn), lambda l, j: (l, 0, j))],
        out_specs=pl.BlockSpec((1, 8, tn), lambda l, j: (l, 0, j)),
        out_shape=jax.ShapeDtypeStruct((DEPTH, 8, 6 * D), F32),
        compiler_params=_params(2), name="ada")(cc, ada_w, ada_b.reshape(DEPTH, 1, 6 * D))


def _mod_kernel(x_ref, sc_ref, sh_ref, o_ref):
    o_ref[0] = (x_ref[0] * (1.0 + sc_ref[0]) + sh_ref[0]).astype(o_ref.dtype)


def _modulate(x, mod, layer):
    return pl.pallas_call(
        _mod_kernel, grid=(NB, S // TM),
        in_specs=[pl.BlockSpec((1, TM, D), lambda b, i: (b, i, 0)), _mod_spec(layer, 1), _mod_spec(layer, 0)],
        out_specs=pl.BlockSpec((1, TM, D), lambda b, i: (b, i, 0)),
        out_shape=jax.ShapeDtypeStruct((NB, S, D), BF16),
        compiler_params=_params(2), name="modulate")(x, mod, mod)


def _swap32(x):
    n = x.shape[1]
    lane = lax.broadcasted_iota(I32, x.shape, 1)
    return jnp.where((lane & 32) == 0, pltpu.roll(x, n - 32, 1), pltpu.roll(x, 32, 1))


def _proj_kernel(a_ref, w_ref, cos_ref, sin_ref, o_ref, wbf_ref, *, rope_cols):
    @pl.when((pl.program_id(1) == 0) & (pl.program_id(2) == 0))
    def _():
        wbf_ref[...] = w_ref[...].astype(BF16)

    acc = jnp.dot(a_ref[0], wbf_ref[...], preferred_element_type=F32)
    if rope_cols:
        xr = acc[:, :rope_cols]
        reps = rope_cols // HD
        cos = jnp.concatenate([cos_ref[...]] * reps, axis=1)
        sin = jnp.concatenate([sin_ref[...]] * reps, axis=1)
        xr = xr * cos + _swap32(xr) * sin
        acc = xr if rope_cols == acc.shape[1] else jnp.concatenate([xr, acc[:, rope_cols:]], axis=1)
    o_ref[0] = acc.astype(o_ref.dtype)


def _proj(h, w, layer, col0, ncols, tn, rope_cols, rope, out_dtype, name):
    assert col0 % tn == 0 and ncols % tn == 0
    off = col0 // tn
    cos_t, sin_t = rope
    return pl.pallas_call(
        functools.partial(_proj_kernel, rope_cols=rope_cols),
        grid=(ncols // tn, NB, S // TMW),
        in_specs=[pl.BlockSpec((1, TMW, D), lambda j, b, i: (b, i, 0)),
                  pl.BlockSpec((None, D, tn), lambda j, b, i: (layer, 0, j + off)),
                  pl.BlockSpec((TMW, HD), lambda j, b, i: (i, 0)),
                  pl.BlockSpec((TMW, HD), lambda j, b, i: (i, 0))],
        out_specs=pl.BlockSpec((1, TMW, tn), lambda j, b, i: (b, i, j)),
        out_shape=jax.ShapeDtypeStruct((NB, S, ncols), out_dtype),
        scratch_shapes=[pltpu.VMEM((D, tn), BF16)],
        compiler_params=_params(3), name=name)(h, w, cos_t, sin_t)


def _rope_tables():
    t = jnp.arange(S)
    nf = HD // 4
    inv = ROPE_BASE ** (-2.0 * jnp.arange(nf, dtype=F32) / (HD // 2))
    lat = (t < L)[:, None]
    cos, sin = [], []
    for pos in (t // GW, t % GW):
        ang = pos.astype(F32)[:, None] * inv[None, :]
        c = jnp.where(lat, jnp.cos(ang), 1.0)
        s = jnp.where(lat, jnp.sin(ang), 0.0)
        cos += [c, c]
        sin += [-s, s]
    return jnp.concatenate(cos, axis=1), jnp.concatenate(sin, axis=1)


def _swa_kernel(sink_ref, q_ref, k0_ref, k1_ref, k2_ref, kc_ref, v0_ref, v1_ref, v2_ref, vc_ref, mask_ref, o_ref):
    mask = mask_ref[0]
    rg = lax.broadcasted_iota(I32, (SWA_G * SWA_W, 1), 0) // SWA_W
    for hk in range(SWA_KV):
        ks = slice(hk * HD, (hk + 1) * HD)
        qs = jnp.concatenate([q_ref[0, :, (hk * SWA_G + g) * HD:(hk * SWA_G + g + 1) * HD] for g in range(SWA_G)],
                             axis=0)
        k = jnp.concatenate([k0_ref[0, :, ks], k1_ref[0, :, ks], k2_ref[0, :, ks], kc_ref[0, :, ks]], axis=0)
        v = jnp.concatenate([v0_ref[0, :, ks], v1_ref[0, :, ks], v2_ref[0, :, ks], vc_ref[0, :, ks]], axis=0)
        s = lax.dot_general(qs, k, (((1,), (1,)), ((), ())), preferred_element_type=F32) * (HD ** -0.5 * LOG2E) + mask
        snk = jnp.zeros((SWA_G * SWA_W, 1), F32)
        for g in range(SWA_G):
            snk = jnp.where(rg == g, sink_ref[hk * SWA_G + g] * LOG2E, snk)
        m = jnp.maximum(jnp.max(s, axis=1, keepdims=True), snk)
        p = jnp.exp2(s - m)
        den = jnp.sum(p, axis=1, keepdims=True) + jnp.exp2(snk - m)
        o = jnp.dot(p.astype(BF16), v, preferred_element_type=F32) / den
        for g in range(SWA_G):
            o_ref[0, :, (hk * SWA_G + g) * HD:(hk * SWA_G + g + 1) * HD] = \
                o[g * SWA_W:(g + 1) * SWA_W].astype(o_ref.dtype)


def _swa_masks():
    r = jnp.arange(SWA_W)[:, None]
    c = jnp.arange(3 * SWA_W + LC)[None, :]
    band = (c - r >= 0) & (c - r <= 2 * SWA_W)
    is_ctx = c >= 3 * SWA_W
    variants = [band & (c >= SWA_W), band, band & (c < 2 * SWA_W), jnp.zeros_like(band)]
    m = jnp.stack([jnp.where(is_ctx | v, 0.0, NEG).astype(F32) for v in variants])
    return jnp.tile(m, (1, SWA_G, 1))


def _swa(q, kv, sink, masks):
    nb = L // SWA_W
    kvw = SWA_KV * HD

    def band(j, part):
        return pl.BlockSpec((1, SWA_W, kvw), lambda b, n: (b, jnp.clip(n + j - 1, 0, nb - 1), part))

    def ctx(part):
        return pl.BlockSpec((1, LC, kvw), lambda b, n: (b, L // LC, part))

    def variant(b, n):
        return (jnp.where(n == 0, 0, jnp.where(n < nb - 1, 1, jnp.where(n == nb - 1, 2, 3))), 0, 0)

    qspec = pl.BlockSpec((1, SWA_W, SWA_H * HD), lambda b, n: (b, n, 0))
    return pl.pallas_call(
        _swa_kernel, grid=(NB, S // SWA_W),
        in_specs=[pl.BlockSpec(memory_space=pltpu.SMEM), qspec,
                  band(0, 0), band(1, 0), band(2, 0), ctx(0), band(0, 1), band(1, 1), band(2, 1), ctx(1),
                  pl.BlockSpec((1,) + masks.shape[1:], variant)],
        out_specs=qspec,
        out_shape=jax.ShapeDtypeStruct((NB, S, SWA_H * HD), BF16),
        compiler_params=_params(2), name="swa")(sink, q, kv, kv, kv, kv, kv, kv, kv, kv, masks)


N_DR = 2 * NA_KH - 1
N_DC = 2 * NA_KW - 1
NA_G = 4
NA_WIN = NA_G + NA_KH
NA_QB = NA_G * GW
NA_KB = NA_WIN * GW // NA_QB


def _na_bias_kernel(rpb_ref, o_ref):
    h = pl.program_id(0)
    shape = (GW, 2 * GW)
    lane = lax.broadcasted_iota(I32, shape, 1)
    qc = lax.broadcasted_iota(I32, shape, 0)
    kc = lane & (GW - 1)
    diff = kc - qc + (NA_KW - 1)
    cs = jnp.clip(qc - NA_KW // 2, 0, GW - NA_KW)
    ok = (kc >= cs) & (kc < cs + NA_KW)
    right = lax.broadcasted_iota(I32, (1, 2 * GW), 1) >= GW
    for dd in range(N_DR + 1):
        acc = jnp.zeros(shape, F32)
        for m in range(N_DC):
            lo = rpb_ref[(h * N_DR + dd - 1) * N_DC + m] if dd >= 1 else 0.0
            hi = rpb_ref[(h * N_DR + dd) * N_DC + m] if dd < N_DR else 0.0
            acc = jnp.where(diff == m, jnp.where(right, hi, lo), acc)
        valid = ok & (right if dd == 0 else (~right if dd == N_DR else True))
        o_ref[0, dd] = jnp.where(valid, acc * LOG2E, NEG)


def _na_bias(rpb):
    return pl.pallas_call(
        _na_bias_kernel, grid=(NA_H,),
        in_specs=[pl.BlockSpec(memory_space=pltpu.SMEM)],
        out_specs=pl.BlockSpec((1, N_DR + 1, GW, 2 * GW), lambda h: (h, 0, 0, 0)),
        out_shape=jax.ShapeDtypeStruct((NA_H, N_DR + 1, GW, 2 * GW), F32),
        compiler_params=_params(1), name="na_bias")(rpb.reshape(-1))


def _na_kernel(*refs):
    q_ref = refs[0]
    k_refs = refs[1:1 + NA_KB]
    v_refs = refs[1 + NA_KB:1 + 2 * NA_KB]
    kc_ref, vc_ref, tz_ref, o_ref = refs[1 + 2 * NA_KB:]
    scale = HD ** -0.5 * LOG2E
    nt = (((1,), (1,)), ((), ()))
    rows = L // GW
    g = pl.program_id(1)
    is_lat = g < rows // NA_G
    r0 = g * NA_G
    ustart = jnp.clip(r0 - NA_KH // 2, 0, rows - NA_WIN)
    lane = lax.broadcasted_iota(I32, (1, NA_WIN * GW), 1)
    masks, dds = [], []
    for a in range(NA_G):
        r = r0 + a
        lo = (jnp.clip(r - NA_KH // 2, 0, rows - NA_KH) - ustart) * GW
        hi = jnp.where(is_lat, lo + NA_KH * GW, lo)
        masks.append((lane >= lo) & (lane < hi))
        dds.append([jnp.clip(ustart + 2 * p - r + NA_KH, 0, N_DR) for p in range(NA_WIN // 2)])
    for h in range(NA_H):
        hs = slice(h * HD, (h + 1) * HD)
        qh = q_ref[0, :, hs]
        kh = jnp.concatenate([r[0, :, hs] for r in k_refs], axis=0)
        vh = jnp.concatenate([r[0, :, hs] for r in v_refs], axis=0)
        s = lax.dot_general(qh, kh, nt, preferred_element_type=F32) * scale
        parts = []
        for a in range(NA_G):
            bias = jnp.concatenate([tz_ref[h, dd] for dd in dds[a]], axis=1)
            parts.append(jnp.where(masks[a], s[a * GW:(a + 1) * GW] + bias, NEG))
        s = jnp.concatenate(parts, axis=0)
        sc = lax.dot_general(qh, kc_ref[0, :, hs], nt, preferred_element_type=F32) * scale
        m = jnp.maximum(jnp.max(s, axis=1, keepdims=True), jnp.max(sc, axis=1, keepdims=True))
        p = jnp.exp2(s - m)
        pc = jnp.exp2(sc - m)
        den = jnp.sum(p, axis=1, keepdims=True) + jnp.sum(pc, axis=1, keepdims=True)
        o = jnp.dot(p.astype(BF16), vh, preferred_element_type=F32)
        o = o + jnp.dot(pc.astype(BF16), vc_ref[0, :, hs], preferred_element_type=F32)
        o_ref[0, :, hs] = (o / den).astype(o_ref.dtype)


def _na(qkv, tz):
    rows = L // GW
    assert LC == NA_QB and (rows - NA_WIN) % NA_G == 0

    def win(j, part):
        return pl.BlockSpec(
            (1, NA_QB, D), lambda b, g: (b, jnp.clip(g * NA_G - NA_KH // 2, 0, rows - NA_WIN) // NA_G + j, part))

    in_specs = [pl.BlockSpec((1, NA_QB, D), lambda b, g: (b, g, 0))]
    in_specs += [win(j, 1) for j in range(NA_KB)] + [win(j, 2) for j in range(NA_KB)]
    in_specs += [pl.BlockSpec((1, LC, D), lambda b, g: (b, L // LC, 1)),
                 pl.BlockSpec((1, LC, D), lambda b, g: (b, L // LC, 2)),
                 pl.BlockSpec(tz.shape, lambda b, g: (0, 0, 0, 0))]
    return pl.pallas_call(
        _na_kernel, grid=(NB, S // NA_QB),
        in_specs=in_specs,
        out_specs=pl.BlockSpec((1, NA_QB, D), lambda b, g: (b, g, 0)),
        out_shape=jax.ShapeDtypeStruct((NB, S, D), BF16),
        compiler_params=_params(2), name="na")(*([qkv] * (3 + 2 * NA_KB)), tz)


def _filter_kernel(z_ref, w1_ref, b1_ref, w2_ref, b2_ref, w3_ref, fr_ref, dl_ref, o_ref):
    hp = lax.Precision.HIGHEST
    z = z_ref[...]
    fr = fr_ref[...]
    h = jnp.sin(fr * (jnp.dot(z, w1_ref[...], precision=hp, preferred_element_type=F32) + b1_ref[...]))
    for i in range(w2_ref.shape[0]):
        h = jnp.sin(fr * (jnp.dot(h, w2_ref[i], precision=hp, preferred_element_type=F32) + b2_ref[i]))
    h = jnp.dot(h.astype(BF16), w3_ref[...].astype(BF16), preferred_element_type=F32)
    decay = jnp.exp(-z[:, 0:1] * dl_ref[...])
    parts = []
    for order in range(2):
        fwd = h[:, (2 * order) * HY:(2 * order + 1) * HY] * decay
        bwd = h[:, (2 * order + 1) * HY:(2 * order + 2) * HY] * decay
        parts += [fwd + bwd, fwd - bwd]
    o_ref[...] = jnp.concatenate(parts, axis=1).astype(o_ref.dtype)


def _filter_embedding(n):
    t = jnp.linspace(0.0, 1.0, n, dtype=F32)[:, None]
    w = (2.0 * math.pi / n) * jnp.arange(n, dtype=F32)[:, None]
    f = jnp.linspace(1e-4, HY_BANDS - 1, HY_BANDS, dtype=F32)[None, :]
    z = jnp.concatenate([t, jnp.cos(f * w), -jnp.sin(f * w)], axis=-1)
    return jnp.pad(z, ((0, 0), (0, HD - HY_EMB)))


def _hyena_filters(n, fw1, fb1, fw2, fb2, fw3, ffreq):
    deltas = jnp.abs(jnp.linspace(math.log(HY_TARGET) / HY_SLOW_DECAY, math.log(HY_TARGET) / HY_FAST_DECAY,
                                  HY, dtype=F32))[None, :]
    w1 = jnp.pad(fw1, ((0, HD - HY_EMB), (0, 0)))
    tm = min(n, 256)
    nout = fw3.shape[1]
    full = lambda a: pl.BlockSpec(a.shape, lambda i: (0,) * a.ndim)
    args = (_filter_embedding(n), w1, fb1.reshape(1, -1), fw2, fb2.reshape(fb2.shape[0], 1, -1), fw3,
            ffreq.reshape(1, -1), deltas)
    return pl.pallas_call(
        _filter_kernel, grid=(n // tm,),
        in_specs=[pl.BlockSpec((tm, HD), lambda i: (i, 0))] + [full(a) for a in args[1:]],
        out_specs=pl.BlockSpec((tm, nout), lambda i: (i, 0)),
        out_shape=jax.ShapeDtypeStruct((n, nout), BF16),
        compiler_params=_params(1), name="hy_filter")(*args)


def _shortconv_kernel(u_ref, w_ref, b_ref, o_ref):
    u = u_ref[0].astype(F32)
    t = lax.broadcasted_iota(I32, (S, 1), 0)
    prev = jnp.where((t == 0) | (t == L), 0.0, pltpu.roll(u, 1, 0))
    nxt = jnp.where((t == L - 1) | (t == S - 1), 0.0, pltpu.roll(u, S - 1, 0))
    w = w_ref[...]
    o_ref[0] = (b_ref[...] + prev * w[0:1] + u * w[1:2] + nxt * w[2:3]).astype(o_ref.dtype)


def _shortconv(u, w, b):
    tc = 256
    nc = u.shape[2]
    return pl.pallas_call(
        _shortconv_kernel, grid=(NB, nc // tc),
        in_specs=[pl.BlockSpec((1, S, tc), lambda b_, c: (b_, 0, c)),
                  pl.BlockSpec((3, tc), lambda b_, c: (0, c)),
                  pl.BlockSpec((1, tc), lambda b_, c: (0, c))],
        out_specs=pl.BlockSpec((1, S, tc), lambda b_, c: (b_, 0, c)),
        out_shape=jax.ShapeDtypeStruct(u.shape, BF16),
        compiler_params=_params(2), name="shortconv")(u, w, b.reshape(1, -1))


def _dft_tables(n):
    nn = 2 * n
    lo = 64
    f = jnp.arange(n, dtype=I32)[:, None]
    angle = lambda m: (m % nn).astype(F32) * (2.0 * math.pi / nn)
    a = angle(f * (jnp.arange(n // lo, dtype=I32) * lo)[None, :])[:, :, None]
    b = angle(f * jnp.arange(lo, dtype=I32)[None, :])[:, None, :]
    c = (jnp.cos(a) * jnp.cos(b) - jnp.sin(a) * jnp.sin(b)).reshape(n, n)
    s = -(jnp.sin(a) * jnp.cos(b) + jnp.cos(a) * jnp.sin(b)).reshape(n, n)
    i = jnp.arange(n, dtype=I32)
    alt = jnp.where(i % 2 == 0, 1.0, -1.0).astype(F32)
    fwd = jnp.concatenate([c, jnp.where(i[:, None] == 0, alt[None, :], s)], axis=0)
    inv_r = c * jnp.where(i == 0, 1.0 / nn, 2.0 / nn).astype(F32)[None, :]
    inv_i = jnp.where(i[None, :] == 0, alt[:, None] / nn, s * (2.0 / nn))
    return fwd.astype(BF16), inv_r.astype(BF16), inv_i.astype(BF16)


HY_TN = 512
HY_NCT = HY // HY_TN


def _filt_dft_kernel(c_ref, s_ref, hs_ref, hd_ref, kr_ref, ki_ref):
    kr_ref[0] = jnp.dot(c_ref[...], hs_ref[...], preferred_element_type=F32)
    ki_ref[0] = jnp.dot(s_ref[...], hd_ref[...], preferred_element_type=F32)

    @pl.when(pl.program_id(0) == 0)
    def _():
        nyq = jnp.dot(s_ref[0:8, :], hs_ref[...], preferred_element_type=F32)
        first = lax.broadcasted_iota(I32, nyq.shape, 0) == 0
        ki_ref[0, 0:8, :] = jnp.where(first, nyq, ki_ref[0, 0:8, :])


def _filt_dft(fwd, hfilt, n):
    tf = min(n, 1024)
    nf = n // tf
    out = pl.BlockSpec((1, tf, HY_TN), lambda f, c: (c // HY_NCT, f, c % HY_NCT))
    return pl.pallas_call(
        _filt_dft_kernel, grid=(nf, 2 * HY_NCT),
        in_specs=[pl.BlockSpec((tf, n), lambda f, c: (f, 0)),
                  pl.BlockSpec((tf, n), lambda f, c: (nf + f, 0)),
                  pl.BlockSpec((n, HY_TN), lambda f, c: (0, (c // HY_NCT) * 2 * HY_NCT + c % HY_NCT)),
                  pl.BlockSpec((n, HY_TN), lambda f, c: (0, (c // HY_NCT) * 2 * HY_NCT + HY_NCT + c % HY_NCT))],
        out_specs=[out, out],
        out_shape=[jax.ShapeDtypeStruct((2, n, HY), F32)] * 2,
        compiler_params=_params(2), name=f"filt_dft{n}")(fwd, fwd, hfilt, hfilt)


def _dft_mul_kernel(z_ref, c_ref, s_ref, kr_ref, ki_ref, yr_ref, yi_ref):
    z = z_ref[0].astype(BF16)
    zr = jnp.dot(c_ref[...], z, preferred_element_type=F32)
    zi = jnp.dot(s_ref[...], z, preferred_element_type=F32)
    kr = kr_ref[...]
    ki = ki_ref[...]
    first = (lax.broadcasted_iota(I32, zr.shape, 0) == 0) & (pl.program_id(0) == 0)
    ii = zi * ki
    yr_ref[0] = (zr * kr - jnp.where(first, 0.0, ii)).astype(yr_ref.dtype)
    yi_ref[0] = jnp.where(first, ii, zr * ki + zi * kr).astype(yi_ref.dtype)


def _dft_mul(z, zrow, zc0, n, fwd, kr, ki, order):
    tf = min(n, 1024)
    nf = n // tf
    out = pl.BlockSpec((1, tf, HY_TN), lambda f, b, c: (b, f, c))
    kspec = lambda: pl.BlockSpec((None, tf, HY_TN), lambda f, b, c: (order, f, c))
    return pl.pallas_call(
        _dft_mul_kernel, grid=(nf, NB, HY_NCT),
        in_specs=[pl.BlockSpec((1, n, HY_TN), lambda f, b, c: (b, zrow, zc0 + c)),
                  pl.BlockSpec((tf, n), lambda f, b, c: (f, 0)),
                  pl.BlockSpec((tf, n), lambda f, b, c: (nf + f, 0)),
                  kspec(), kspec()],
        out_specs=[out, out],
        out_shape=[jax.ShapeDtypeStruct((NB, n, HY), BF16)] * 2,
        compiler_params=_params(3), name=f"dft_mul{n}")(z, fwd, fwd, kr, ki)


def _idft_gate_kernel(ar_ref, ai_ref, yr_ref, yi_ref, z_ref, p_ref, bias_ref, *o_refs):
    y = jnp.dot(ar_ref[...], yr_ref[0], preferred_element_type=F32)
    y = y + jnp.dot(ai_ref[...], yi_ref[0], preferred_element_type=F32)
    out = p_ref[0].astype(F32) * (y + z_ref[0].astype(F32) * bias_ref[0])
    for o_ref in o_refs:
        o_ref[0] = out.astype(o_ref.dtype)


def _idft_gate(yr, yi, inv_r, inv_i, n, z, z_r0, z_c0, p, p_r0, p_c0, bias, order, out_dtype):
    tm = min(n, 1024)
    return pl.pallas_call(
        _idft_gate_kernel, grid=(n // tm, NB, HY_NCT),
        in_specs=[pl.BlockSpec((tm, n), lambda t, b, c: (t, 0)),
                  pl.BlockSpec((tm, n), lambda t, b, c: (t, 0)),
                  pl.BlockSpec((1, n, HY_TN), lambda t, b, c: (b, 0, c)),
                  pl.BlockSpec((1, n, HY_TN), lambda t, b, c: (b, 0, c)),
                  pl.BlockSpec((1, tm, HY_TN), lambda t, b, c: (b, z_r0 + t, z_c0 + c)),
                  pl.BlockSpec((1, tm, HY_TN), lambda t, b, c: (b, p_r0 + t, p_c0 + c)),
                  pl.BlockSpec((1, 1, HY_TN), lambda t, b, c: (order, 0, c))],
        out_specs=pl.BlockSpec((1, tm, HY_TN), lambda t, b, c: (b, t, c)),
        out_shape=jax.ShapeDtypeStruct((NB, n, HY), out_dtype),
        compiler_params=_params(3), name=f"idft_gate{n}_{order}")(inv_r, inv_i, yr, yi, z, p, bias)


def _hyena(u, conv_w, conv_b, filt, hy_bias, tables):
    sc = _shortconv(u, conv_w, conv_b)
    bias = hy_bias.reshape(2, 1, HY)
    outs = []
    for n in (L, LC):
        zrow = 0 if n == L else L // LC
        r0 = zrow * (n // min(n, 1024))
        fwd, inv_r, inv_i = tables[n]
        kr, ki = _filt_dft(fwd, _hyena_filters(n, *filt), n)
        yr, yi = _dft_mul(sc, zrow, 2 * HY_NCT, n, fwd, kr, ki, 0)
        z1 = _idft_gate(yr, yi, inv_r, inv_i, n, sc, r0, 2 * HY_NCT, sc, r0, 0, bias, 0, BF16)
        yr, yi = _dft_mul(z1, 0, 0, n, fwd, kr, ki, 1)
        outs.append(_idft_gate(yr, yi, inv_r, inv_i, n, z1, 0, 0, sc, r0, HY_NCT, bias, 1, BF16))
    return outs


def _resid_ln(x, y, gate, g, b):
    v = DN_ALPHA * x + gate * y
    mu = jnp.mean(v, axis=-1, keepdims=True)
    vc = v - mu
    var = jnp.mean(vc * vc, axis=-1, keepdims=True)
    return vc * lax.rsqrt(var + LN_EPS) * g + b


def _wout_ln_kernel(a1a_ref, a1b_ref, a1c_ref, a1x_ref, a2_ref, w_ref, x_ref, gl_ref, gc_ref, g_ref, b_ref,
                    scl_ref, scc_ref, shl_ref, shc_ref, wrh_ref, wrl_ref, xo_ref, hm_ref, aff_ref):
    half = a2_ref.shape[2]
    last_tile = pl.program_id(1) == pl.num_programs(1) - 1
    a1_refs = (a1a_ref, a1b_ref, a1c_ref)
    n_sub = len(a1_refs)

    def project(s):
        a1 = jnp.where(last_tile, a1x_ref[0], a1_refs[s][0]) if s == n_sub - 1 else a1_refs[s][0]
        y = jnp.dot(a1, w_ref[0:half, :], preferred_element_type=F32)
        return y + jnp.dot(a2_ref[0, s * TM:(s + 1) * TM, :], w_ref[half:2 * half, :], preferred_element_type=F32)

    for s in range(n_sub):
        rows = slice(s * TM, (s + 1) * TM)
        if s == n_sub - 1:
            pick = lambda lat_ref, ctx_ref: jnp.where(last_tile, ctx_ref[0], lat_ref[0])
        else:
            pick = lambda lat_ref, ctx_ref: lat_ref[0]
        xn = _resid_ln(x_ref[0, rows, :], project(s), pick(gl_ref, gc_ref), g_ref[0], b_ref[0])
        xo_ref[0, rows, :] = xn
        hm = xn * (1.0 + pick(scl_ref, scc_ref)) + pick(shl_ref, shc_ref)
        hm_hi = hm.astype(BF16)
        hm_ref[0, rows, :] = hm_hi
        hm_lo = (hm - hm_hi.astype(F32)).astype(BF16)
        logits = jnp.dot(hm_hi, wrh_ref[...], preferred_element_type=F32)
        logits = logits + jnp.dot(hm_lo, wrh_ref[...], preferred_element_type=F32)
        logits = logits + jnp.dot(hm_hi, wrl_ref[...], preferred_element_type=F32)
        e = jnp.exp(logits - jnp.max(logits, axis=-1, keepdims=True))
        aff_ref[0, rows, :] = e / jnp.sum(e, axis=-1, keepdims=True)


def _wout_ln(a1_lat, a1_ctx, a1_ctx_blk, c1, a2, c2, w_bf, x, mod, layer, lng, lnb, w_router):
    half = D // 2
    sub = TMW // TM
    n_lat_blk = a1_lat.shape[1] // TM
    tile = lambda: pl.BlockSpec((1, TMW, D), lambda b, i: (b, i, 0))
    vec = lambda idx: pl.BlockSpec((1, 1, D), idx)
    lat = lambda k: vec(lambda b, i: ((layer * 8 + b) * 6 + k, 0, 0))
    ctx = lambda k: vec(lambda b, i: ((layer * 8 + NB) * 6 + k, 0, 0))
    a1_blk = lambda s: pl.BlockSpec((1, TM, half), lambda b, i: (b, jnp.minimum(i * sub + s, n_lat_blk - 1), c1))
    wr_hi = w_router.astype(BF16)
    wr_lo = (w_router - wr_hi.astype(F32)).astype(BF16)
    return pl.pallas_call(
        _wout_ln_kernel, grid=(NB, S // TMW),
        in_specs=[a1_blk(0), a1_blk(1), a1_blk(2),
                  pl.BlockSpec((1, TM, half), lambda b, i: (b, a1_ctx_blk, c1)),
                  pl.BlockSpec((1, TMW, half), lambda b, i: (b, i, c2)),
                  pl.BlockSpec((D, D), lambda b, i: (0, 0), pipeline_mode=pl.Buffered(1)),
                  tile(), lat(2), ctx(2),
                  vec(lambda b, i: (2 * layer, 0, 0)), vec(lambda b, i: (2 * layer, 0, 0)),
                  lat(4), ctx(4), lat(3), ctx(3),
                  pl.BlockSpec((D, NE), lambda b, i: (0, 0)),
                  pl.BlockSpec((D, NE), lambda b, i: (0, 0))],
        out_specs=[tile(), tile(), pl.BlockSpec((1, TMW, NE), lambda b, i: (b, i, 0))],
        out_shape=[jax.ShapeDtypeStruct((NB, S, D), F32), jax.ShapeDtypeStruct((NB, S, D), BF16),
                   jax.ShapeDtypeStruct((NB, S, NE), F32)],
        compiler_params=_params(2), name="wout_ln")(
            a1_lat, a1_lat, a1_lat, a1_ctx, a2, w_bf, x, mod, mod, lng, lnb, mod, mod, mod, mod, wr_hi, wr_lo)


def _route_kernel(a_ref, o_ref, tri_ref):
    rb = 256
    for r in range(L // rb):
        s_i = lax.broadcasted_iota(I32, (rb, L), 0) + r * rb
        t_i = lax.broadcasted_iota(I32, (rb, L), 1)
        tri_ref[r * rb:(r + 1) * rb, :] = jnp.where(s_i < t_i, 1.0, 0.0).astype(BF16)

    def select(a, cap, base):
        n = a.shape[1]

        def body(i, prefix):
            cand = prefix | jnp.left_shift(jnp.int32(1), 30 - i)
            cnt = jnp.sum(jnp.where(a >= lax.bitcast_convert_type(cand, F32), 1.0, 0.0), axis=1, keepdims=True)
            return jnp.where(cnt >= cap, cand, prefix)

        kth = lax.bitcast_convert_type(lax.fori_loop(0, 31, body, jnp.zeros((a.shape[0], 1), I32)), F32)
        gt = a > kth
        eq = a == kth
        n_gt = jnp.sum(jnp.where(gt, 1.0, 0.0), axis=1, keepdims=True)
        tri = tri_ref[0:n, 0:n]
        before = jnp.dot(jnp.where(eq, 1.0, 0.0).astype(BF16), tri, preferred_element_type=F32)
        sel = gt | (eq & (before < cap - n_gt))
        pos = jnp.dot(jnp.where(sel, 1.0, 0.0).astype(BF16), tri, preferred_element_type=F32)
        return jnp.where(sel, pos.astype(I32) + base, -1)

    a = a_ref[...]
    o_ref[...] = jnp.concatenate([select(a[:, :L], CAP_L, 0), select(a[:, L:], CAP_C, CAP_L)], axis=1)


def _route(aff_t):
    return pl.pallas_call(
        _route_kernel, grid=(1,),
        in_specs=[pl.BlockSpec((NB * NE, S), lambda i: (0, 0))],
        out_specs=pl.BlockSpec((NB * NE, S), lambda i: (0, 0)),
        out_shape=jax.ShapeDtypeStruct((NB * NE, S), I32),
        scratch_shapes=[pltpu.VMEM((L, L), BF16)],
        compiler_params=_params(1), name="route")(aff_t.reshape(NB * NE, S)).reshape(NB, NE, S)


GATHER_EG = 4


def _gather_kernel(slot_ref, aff_ref, h_ref, xs_ref, g_ref):
    for r0, cap, c0, n in ((0, CAP_L, 0, L), (CAP_L, CAP_C, L, LC)):
        want = lax.broadcasted_iota(I32, (cap, n), 0) + r0
        hits = []
        for e in range(GATHER_EG):
            hit = slot_ref[0, e][:, c0:c0 + n] == want
            hits.append(jnp.where(hit, 1.0, 0.0).astype(BF16))
            g = jnp.sum(jnp.where(hit, aff_ref[0, e][:, c0:c0 + n], 0.0), axis=1, keepdims=True)
            g_ref[0, e, r0:r0 + cap, :] = jnp.broadcast_to(g, (cap, HD))
        xs = jnp.dot(jnp.concatenate(hits, axis=0), h_ref[0, c0:c0 + n, :], preferred_element_type=F32)
        for e in range(GATHER_EG):
            xs_ref[0, e, r0:r0 + cap, :] = xs[e * cap:(e + 1) * cap].astype(xs_ref.dtype)


def _gather(slot, aff_t, hm):
    row = lambda: pl.BlockSpec((1, GATHER_EG, 1, S), lambda b, e: (b, e, 0, 0))
    return pl.pallas_call(
        _gather_kernel, grid=(NB, NE // GATHER_EG),
        in_specs=[row(), row(), pl.BlockSpec((1, S, D), lambda b, e: (b, 0, 0))],
        out_specs=[pl.BlockSpec((1, GATHER_EG, CAP, D), lambda b, e: (b, e, 0, 0)),
                   pl.BlockSpec((1, GATHER_EG, CAP, HD), lambda b, e: (b, e, 0, 0))],
        out_shape=[jax.ShapeDtypeStruct((NB, NE, CAP, D), BF16), jax.ShapeDtypeStruct((NB, NE, CAP, HD), F32)],
        compiler_params=_params(2), name="moe_gather")(
            slot.reshape(NB, NE, 1, S), aff_t.reshape(NB, NE, 1, S), hm)


FFN_SPLIT = 2


def _ffn_kernel(xs_ref, g_ref, w1_ref, w3_ref, w2_ref, ol_ref, oc_ref, hmid_ref):
    f = pl.program_id(1)
    tf = FF // FFN_SPLIT

    @pl.when(f < FFN_SPLIT)
    def _():
        x = xs_ref[:, 0].reshape(NB * CAP, D)
        a = jnp.dot(x, w1_ref[0].astype(BF16), preferred_element_type=F32)
        u = jnp.dot(x, w3_ref[0].astype(BF16), preferred_element_type=F32)
        hmid_ref[:, pl.ds(pl.multiple_of(f * tf, tf), tf)] = (a * jax.nn.sigmoid(a) * u).astype(BF16)

    @pl.when(f >= FFN_SPLIT)
    def _():
        y = jnp.dot(hmid_ref[...], w2_ref[0].astype(BF16), preferred_element_type=F32)
        g = g_ref[:, 0].reshape(NB * CAP, HD)[:, 0:1]
        y = (y * g).astype(ol_ref.dtype).reshape(NB, CAP, y.shape[1])
        ol_ref[:, 0] = y[:, :CAP_L]
        oc_ref[:, 0] = y[:, CAP_L:]


def _ffn(xs, gate, w1, w3, w2, layer):
    tf = FF // FFN_SPLIT
    tn = D // FFN_SPLIT
    hid = lambda e, f: (layer, e, 0, jnp.minimum(f, FFN_SPLIT - 1))
    col = lambda f: jnp.maximum(f - FFN_SPLIT, 0)
    out = lambda cap: pl.BlockSpec((NB, 1, cap, tn), lambda e, f: (0, e, 0, col(f)))

    def rows_index(e, f):
        return (0, jnp.where(f < FFN_SPLIT, e, jnp.minimum(e + 1, NE - 1)), 0, 0)

    def w1_index(e, f):
        hidden = f < FFN_SPLIT
        return (layer, jnp.where(hidden, e, jnp.minimum(e + 1, NE - 1)), 0, jnp.where(hidden, f, 0))

    def w2_index(e, f):
        hidden = f < FFN_SPLIT
        return (layer, jnp.where(hidden, jnp.maximum(e - 1, 0), e), 0, jnp.where(hidden, FFN_SPLIT - 1, col(f)))

    return pl.pallas_call(
        _ffn_kernel, grid=(NE, 2 * FFN_SPLIT),
        in_specs=[pl.BlockSpec((NB, 1, CAP, D), rows_index),
                  pl.BlockSpec((NB, 1, CAP, HD), lambda e, f: (0, e, 0, 0)),
                  pl.BlockSpec((None, 1, D, tf), w1_index),
                  pl.BlockSpec((None, 1, D, tf), hid),
                  pl.BlockSpec((None, 1, FF, tn), w2_index)],
        out_specs=[out(CAP_L), out(CAP_C)],
        out_shape=[jax.ShapeDtypeStruct((NB, NE, CAP_L, D), BF16), jax.ShapeDtypeStruct((NB, NE, CAP_C, D), BF16)],
        scratch_shapes=[pltpu.VMEM((NB * CAP, FF), BF16)],
        compiler_params=_params(2), name="moe_ffn")(xs, gate, w1, w3, w2)


def _combine_ln_kernel(*refs, last):
    if last:
        slot_ref, yl_ref, x_ref, gate_ref, g_ref, b_ref, xo_ref, acc_ref = refs
    else:
        slot_ref, yl_ref, yc_ref, x_ref, gate_ref, g_ref, b_ref, sc_ref, sh_ref, xo_ref, hn_ref, acc_ref = refs
    st = slot_ref[0]

    def scatter(y_ref, r0, cap):
        want = lax.broadcasted_iota(I32, (TM, cap), 1) + r0
        hits = [jnp.where(st[:, e:e + 1] == want, 1.0, 0.0).astype(BF16) for e in range(NE)]
        if cap % HD == 0:
            acc_ref[...] = jnp.dot(jnp.concatenate(hits, axis=1), y_ref[0].reshape(NE * cap, D),
                                   preferred_element_type=F32)
        else:
            acc = None
            for e in range(NE):
                part = jnp.dot(hits[e], y_ref[0, e], preferred_element_type=F32)
                acc = part if acc is None else acc + part
            acc_ref[...] = acc

    if last:
        scatter(yl_ref, 0, CAP_L)
    else:
        is_lat = pl.program_id(1) < N_LAT_TILES
        pl.when(is_lat)(lambda: scatter(yl_ref, 0, CAP_L))
        pl.when(jnp.logical_not(is_lat))(lambda: scatter(yc_ref, CAP_L, CAP_C))
    xn = _resid_ln(x_ref[0], acc_ref[...], gate_ref[0], g_ref[0], b_ref[0])
    xo_ref[0] = xn
    if not last:
        hn_ref[0] = (xn * (1.0 + sc_ref[0]) + sh_ref[0]).astype(hn_ref.dtype)


def _combine_ln(slot_t, yl, yc, x, mod, layer, lng, lnb, last):
    rows = L if last else S
    tile = lambda: pl.BlockSpec((1, TM, D), lambda b, i: (b, i, 0))
    ln_vec = lambda: pl.BlockSpec((1, 1, D), lambda b, i: (2 * layer + 1, 0, 0))
    in_specs = [pl.BlockSpec((1, TM, NE), lambda b, i: (b, i, 0)),
                pl.BlockSpec((1, NE, CAP_L, D), lambda b, i: (b, 0, 0, 0), pipeline_mode=pl.Buffered(1))]
    args = [slot_t, yl]
    if not last:
        in_specs.append(pl.BlockSpec((1, NE, CAP_C, D), lambda b, i: (b, 0, 0, 0)))
        args.append(yc)
    in_specs += [tile(), _mod_spec(layer, 5), ln_vec(), ln_vec()]
    args += [x, mod, lng, lnb]
    out_specs = [tile()]
    out_shape = [jax.ShapeDtypeStruct((NB, rows, D), F32)]
    if not last:
        in_specs += [_mod_spec(layer + 1, 1), _mod_spec(layer + 1, 0)]
        args += [mod, mod]
        out_specs.append(tile())
        out_shape.append(jax.ShapeDtypeStruct((NB, rows, D), BF16))
    return pl.pallas_call(
        functools.partial(_combine_ln_kernel, last=last), grid=(NB, rows // TM),
        in_specs=in_specs, out_specs=out_specs, out_shape=out_shape,
        scratch_shapes=[pltpu.VMEM((TM, D), F32)],
        compiler_params=_params(2), name="moe_combine_ln")(*args)


def _moe(hm, aff, x, mod, layer, lng, lnb, w1, w3, w2, last):
    aff_t = jnp.swapaxes(aff, 1, 2)
    slot = _route(aff_t)
    xs, gate = _gather(slot, aff_t, hm)
    yl, yc = _ffn(xs, gate, w1, w3, w2, layer)
    return _combine_ln(jnp.swapaxes(slot, 1, 2), yl, yc, x, mod, layer, lng, lnb, last)


def kernel(x, c, ctx, c_ctx, ada_w, ada_b, ln_g, ln_b, ev_w_in, ev_w_out, hy_conv_w, hy_conv_b, hy_f_w1, hy_f_b1,
           hy_f_w2, hy_f_b2, hy_f_w3, hy_f_freq, hy_bias, swa_sink, od_w_in, od_w_out, na_rpb, moe_w_router,
           moe_w1, moe_w3, moe_w2):
    assert x.shape == (NB, L, D) and ctx.shape == (NB, LC, D)
    xa = jnp.concatenate([x, ctx], axis=1)
    cc = jnp.concatenate([c, c_ctx[None, :], jnp.zeros((8 - NB - 1, D), F32)], axis=0)
    mod = _ada(cc, ada_w, ada_b).reshape(DEPTH * 8 * 6, 1, D)
    lng = ln_g.reshape(DEPTH * 2, 1, D)
    lnb = ln_b.reshape(DEPTH * 2, 1, D)
    rope = _rope_tables()
    tables = {L: _dft_tables(L), LC: _dft_tables(LC)}
    swa_masks = _swa_masks()
    h = _modulate(xa, mod, 0)
    for l in range(DEPTH):
        i = l // 2
        last = l == DEPTH - 1
        if l % 2 == 0:
            u = _proj(h, ev_w_in, i, 0, EV_Q0, 1024, 0, rope, BF16, "proj_hy")
            q = _proj(h, ev_w_in, i, EV_Q0, SWA_H * HD, 1024, SWA_H * HD, rope, BF16, "proj_q")
            kv = _proj(h, ev_w_in, i, EV_K0, 2 * SWA_KV * HD, 512, SWA_KV * HD, rope, BF16, "proj_kv")
            att = _swa(q, kv, swa_sink[i], swa_masks)
            filt = (hy_f_w1[i], hy_f_b1[i], hy_f_w2[i], hy_f_b2[i], hy_f_w3[i], hy_f_freq[i])
            z_lat, z_ctx = _hyena(u, hy_conv_w[i], hy_conv_b[i], filt, hy_bias[i], tables)
            xa, hm, aff = _wout_ln(z_lat, z_ctx, 0, 0, att, 0, ev_w_out[i].astype(BF16), xa, mod, l, lng, lnb,
                                   moe_w_router[l])
        else:
            qkv = _proj(h, od_w_in, i, 0, 3 * D, 1024, 0, rope, BF16, "proj_qkv")
            att = _na(qkv, _na_bias(na_rpb[i]))
            xa, hm, aff = _wout_ln(att, att, L // TM, 0, att, 1, od_w_out[i].astype(BF16), xa, mod, l, lng, lnb,
                                   moe_w_router[l])
        out = _moe(hm, aff, xa, mod, l, lng, lnb, moe_w1, moe_w3, moe_w2, last)
        if last:
            return out[0]
        xa, h = out
```
